```python
import jax, jax.numpy as jnp
from jax import lax
import numpy as np

D_MODEL = 2048
BATCH = 2
SEQ = 8192
DEPTH = 1
DEC_BATCH = 1
DEC_SEQ = 8192
PAST_LEN = 128

D_RWKV = D_MODEL // 2
D_CONV = D_MODEL - D_RWKV
HEAD_DIM = 64
N_HEADS = D_RWKV // HEAD_DIM
DECAY_LORA = 64
AAA_LORA = 64
GATE_LORA = 160
CONV_WIDTH = 31
CONV_HALF = CONV_WIDTH // 2
D_FF = 4 * D_MODEL
RMS_EPS = 1e-6
LN_EPS = 1e-5
GN_EPS = 64e-5
L2_EPS = 1e-12
RWKV_COLS = 3 * D_RWKV + 2 * DECAY_LORA + 2 * AAA_LORA + GATE_LORA
IN_COLS = RWKV_COLS + 2 * D_CONV

kernel_name = "hymba_rwkv7_conformer_bidir_encoder"


def _rms(x, g):
    xf = x.astype(jnp.float32)
    y = xf * lax.rsqrt(jnp.mean(xf * xf, axis=-1, keepdims=True) + RMS_EPS)
    return (y * g.astype(jnp.float32)).astype(x.dtype)


def _centred_shift(z, mu_prev, mu_next):
    zp = jnp.pad(z, ((0, 0), (1, 0), (0, 0)))[:, :-1]
    zn = jnp.pad(z, ((0, 0), (0, 1), (0, 0)))[:, 1:]
    return z + mu_prev * (zp - z) + mu_next * (zn - z)


def _wkv_scan(r, w, k, v, kk, a, reverse):
    B, T, H, N = r.shape

    def step(S, inp):
        r_t, w_t, k_t, v_t, kk_t, a_t = inp
        sa = jnp.einsum('bhvk,bhk->bhv', S, kk_t)
        S = (S * w_t[:, :, None, :]
             - jnp.einsum('bhv,bhk->bhvk', sa, kk_t * a_t)
             + jnp.einsum('bhv,bhk->bhvk', v_t, k_t))
        y = jnp.einsum('bhvk,bhk->bhv', S, r_t)
        return S, y

    xs = tuple(jnp.swapaxes(t, 0, 1) for t in (r, w, k, v, kk, a))
    S0 = jnp.zeros((B, H, N, N), jnp.float32)
    _, y = lax.scan(step, S0, xs, reverse=reverse)
    return jnp.swapaxes(y, 0, 1)


def _heads(t):
    B, T, _ = t.shape
    return t.reshape(B, T, N_HEADS, HEAD_DIM)


def _rwkv_direction(r, k, v, kk, xw, xa, w0, w2, a0, a2, k_a, reverse):
    w_log = -jax.nn.softplus(-(w0 + jnp.tanh(xw) @ w2)) - 0.5
    w = jnp.exp(-jnp.exp(w_log))
    a = jax.nn.sigmoid(a0 + xa @ a2)
    k_mod = k * (1.0 + (a - 1.0) * k_a)
    return _wkv_scan(_heads(r), _heads(w), _heads(k_mod), _heads(v), kk, _heads(a), reverse)


def _layer(x, g_pre_mix, w_in, mu_prev, mu_next, w0_f, w2_f, w0_b, w2_b,
           a0_f, a2_f, a0_b, a2_b, g2, k_k, k_a, r_k, gn_w, gn_b,
           dw_w, dw_b, cln_w, cln_b, w_out, g_post_mix, g_pre_mlp,
           w_up, w_down, g_post_mlp):
    B, T, _ = x.shape
    h = _rms(x, g_pre_mix)
    z = h @ w_in
    z_rwkv = _centred_shift(z[..., :RWKV_COLS], mu_prev, mu_next).astype(jnp.float32)
    z_conv = z[..., RWKV_COLS:]

    o = 0
    r = z_rwkv[..., o:o + D_RWKV]; o += D_RWKV
    k = z_rwkv[..., o:o + D_RWKV]; o += D_RWKV
    v = z_rwkv[..., o:o + D_RWKV]; o += D_RWKV
    xw_f = z_rwkv[..., o:o + DECAY_LORA]; o += DECAY_LORA
    xw_b = z_rwkv[..., o:o + DECAY_LORA]; o += DECAY_LORA
    xa_f = z_rwkv[..., o:o + AAA_LORA]; o += AAA_LORA
    xa_b = z_rwkv[..., o:o + AAA_LORA]; o += AAA_LORA
    xg = z_rwkv[..., o:o + GATE_LORA]

    f32 = lambda t: t.astype(jnp.float32)
    g = jax.nn.sigmoid(xg) @ f32(g2)
    kk = _heads(k * f32(k_k))
    kk = kk / jnp.maximum(jnp.sqrt(jnp.sum(kk * kk, axis=-1, keepdims=True)), L2_EPS)

    y_f = _rwkv_direction(r, k, v, kk, xw_f, xa_f, f32(w0_f), f32(w2_f), f32(a0_f), f32(a2_f), f32(k_a), False)
    y_b = _rwkv_direction(r, k, v, kk, xw_b, xa_b, f32(w0_b), f32(w2_b), f32(a0_b), f32(a2_b), f32(k_a), True)
    y = y_f + y_b

    mu = jnp.mean(y, axis=-1, keepdims=True)
    var = jnp.mean(jnp.square(y - mu), axis=-1, keepdims=True)
    y = ((y - mu) * lax.rsqrt(var + GN_EPS)).reshape(B, T, D_RWKV) * f32(gn_w) + f32(gn_b)
    rh, kh, vh = _heads(r), _heads(k), _heads(v)
    bonus = (jnp.sum(rh * kh * f32(r_k), axis=-1, keepdims=True) * vh).reshape(B, T, D_RWKV)
    o_rwkv = ((y + bonus) * g).astype(x.dtype)

    u = z_conv[..., :D_CONV] * jax.nn.sigmoid(z_conv[..., D_CONV:])
    c = lax.conv_general_dilated(u, dw_w[:, None, :], window_strides=(1,),
                                 padding=((CONV_HALF, CONV_HALF),),
                                 dimension_numbers=('NWC', 'WIO', 'NWC'),
                                 feature_group_count=D_CONV) + dw_b
    cf = c.astype(jnp.float32)
    cm = jnp.mean(cf, axis=-1, keepdims=True)
    cv = jnp.mean(jnp.square(cf - cm), axis=-1, keepdims=True)
    cf = (cf - cm) * lax.rsqrt(cv + LN_EPS) * f32(cln_w) + f32(cln_b)
    o_conv = jax.nn.silu(cf).astype(x.dtype)

    mix = jnp.concatenate([o_rwkv, o_conv], axis=-1) @ w_out
    x = x + _rms(mix, g_post_mix)

    hm = _rms(x, g_pre_mlp)
    ff = jnp.square(jax.nn.relu(hm @ w_up)) @ w_down
    return x + _rms(ff, g_post_mlp)


def _trunk(x, weights):
    for l in range(DEPTH):
        x = _layer(x, *[p[l] for p in weights])
    return x


def setup_inputs(seed: int = 0) -> dict:
    key = jax.random.key(seed)
    ks = jax.random.split(key, 32)
    nrm = lambda k, shape, s: jax.random.normal(k, shape, jnp.float32) * s
    L = DEPTH
    return {
        "x_prompt": nrm(ks[0], (BATCH, SEQ, D_MODEL), 1.0),
        "x_sample": nrm(ks[1], (DEC_BATCH, DEC_SEQ, D_MODEL), 1.0),
        "g_pre_mix": 1.0 + nrm(ks[2], (L, D_MODEL), 0.02),
        "w_in": nrm(ks[3], (L, D_MODEL, IN_COLS), D_MODEL ** -0.5),
        "mu_prev": jax.random.uniform(ks[4], (L, RWKV_COLS), jnp.float32, 0.0, 0.5),
        "mu_next": jax.random.uniform(ks[5], (L, RWKV_COLS), jnp.float32, 0.0, 0.5),
        "w0_f": jax.random.uniform(ks[6], (L, D_RWKV), jnp.float32, -1.5, 0.5),
        "w2_f": nrm(ks[7], (L, DECAY_LORA, D_RWKV), 0.1 * DECAY_LORA ** -0.5),
        "w0_b": jax.random.uniform(ks[8], (L, D_RWKV), jnp.float32, -1.5, 0.5),
        "w2_b": nrm(ks[9], (L, DECAY_LORA, D_RWKV), 0.1 * DECAY_LORA ** -0.5),
        "a0_f": nrm(ks[10], (L, D_RWKV), 0.1),
        "a2_f": nrm(ks[11], (L, AAA_LORA, D_RWKV), 0.1 * AAA_LORA ** -0.5),
        "a0_b": nrm(ks[12], (L, D_RWKV), 0.1),
        "a2_b": nrm(ks[13], (L, AAA_LORA, D_RWKV), 0.1 * AAA_LORA ** -0.5),
        "g2": nrm(ks[14], (L, GATE_LORA, D_RWKV), GATE_LORA ** -0.5),
        "k_k": 0.85 + nrm(ks[15], (L, D_RWKV), 0.02),
        "k_a": 1.0 + nrm(ks[16], (L, D_RWKV), 0.02),
        "r_k": nrm(ks[17], (L, N_HEADS, HEAD_DIM), 0.1),
        "gn_w": 1.0 + nrm(ks[18], (L, D_RWKV), 0.02),
        "gn_b": nrm(ks[19], (L, D_RWKV), 0.01),
        "dw_w": nrm(ks[20], (L, CONV_WIDTH, D_CONV), CONV_WIDTH ** -0.5),
        "dw_b": nrm(ks[21], (L, D_CONV), 0.01),
        "cln_w": 1.0 + nrm(ks[22], (L, D_CONV), 0.02),
        "cln_b": nrm(ks[23], (L, D_CONV), 0.01),
        "w_out": nrm(ks[24], (L, D_MODEL, D_MODEL), D_MODEL ** -0.5),
        "g_post_mix": 1.0 + nrm(ks[25], (L, D_MODEL), 0.02),
        "g_pre_mlp": 1.0 + nrm(ks[26], (L, D_MODEL), 0.02),
        "w_up": nrm(ks[27], (L, D_MODEL, D_FF), D_MODEL ** -0.5),
        "w_down": nrm(ks[28], (L, D_FF, D_MODEL), D_FF ** -0.5),
        "g_post_mlp": 1.0 + nrm(ks[29], (L, D_MODEL), 0.02),
    }


def reference(x_prompt, x_sample, g_pre_mix, w_in, mu_prev, mu_next, w0_f, w2_f,
              w0_b, w2_b, a0_f, a2_f, a0_b, a2_b, g2, k_k, k_a, r_k, gn_w, gn_b,
              dw_w, dw_b, cln_w, cln_b, w_out, g_post_mix, g_pre_mlp, w_up,
              w_down, g_post_mlp):
    weights = (g_pre_mix, w_in, mu_prev, mu_next, w0_f, w2_f, w0_b, w2_b,
               a0_f, a2_f, a0_b, a2_b, g2, k_k, k_a, r_k, gn_w, gn_b,
               dw_w, dw_b, cln_w, cln_b, w_out, g_post_mix, g_pre_mlp,
               w_up, w_down, g_post_mlp)
    y_prompt = _trunk(x_prompt, weights)
    y_sample = _trunk(x_sample, weights)
    return (y_prompt, y_sample)
```

```python
import functools
import math

import jax
import jax.numpy as jnp
from jax import lax
from jax.experimental import pallas as pl
from jax.experimental.pallas import tpu as pltpu

F32 = jnp.float32
BF16 = jnp.bfloat16

D_MODEL = 2048
D_RWKV = 1024
D_CONV = 1024
HEAD_DIM = 64
N_HEADS = 16
DECAY_LORA = 64
AAA_LORA = 64
GATE_LORA = 160
CONV_WIDTH = 31
CONV_HALF = CONV_WIDTH // 2
D_FF = 4 * D_MODEL
RMS_EPS = 1e-6
LN_EPS = 1e-5
GN_EPS = 64e-5
L2_EPS = 1e-12

LORA_COLS = 2 * DECAY_LORA + 2 * AAA_LORA + GATE_LORA
LORA_PAD = 512
ZR_COLS = 3 * D_RWKV + LORA_PAD
ZC_COLS = 2 * D_CONV

CHUNK = 64
GROUP = 256
N_GROUPS = D_RWKV // GROUP
HALO = 16

VMEM_LIMIT = 56 * 1024 * 1024


def _cparams(sem):
    return pltpu.CompilerParams(dimension_semantics=sem, vmem_limit_bytes=VMEM_LIMIT)


def _dot(a, b):
    return jnp.dot(a, b, preferred_element_type=F32)


def _dot_nt(a, b):
    return lax.dot_general(a, b, (((1,), (1,)), ((), ())), preferred_element_type=F32)


def _dot_tn(a, b):
    return lax.dot_general(a, b, (((0,), (0,)), ((), ())), preferred_element_type=F32)


def _split2(x):
    hi = x.astype(BF16)
    lo = (x - hi.astype(F32)).astype(BF16)
    return hi, lo


def _split3(x):
    hi = x.astype(BF16)
    r1 = x - hi.astype(F32)
    mid = r1.astype(BF16)
    lo = (r1 - mid.astype(F32)).astype(BF16)
    return hi, mid, lo


def _seg_sum(x, e_ref):
    hi, lo = _split2(x)
    e = e_ref[...]
    return _dot(hi, e) + _dot(lo, e)


def _inproj_kernel(n_rw_tiles, x_ref, g_ref, w_ref, zr_ref, zc_ref, h_scr):
    j = pl.program_id(1)

    @pl.when(j == 0)
    def _():
        x = x_ref[...]
        ms = jnp.mean(x * x, axis=-1, keepdims=True)
        h_scr[...] = (x * lax.rsqrt(ms + RMS_EPS) * g_ref[...]).astype(BF16)

    z = _dot(h_scr[...], w_ref[...])

    @pl.when(j < n_rw_tiles)
    def _():
        zr_ref[...] = z

    @pl.when(j >= n_rw_tiles)
    def _():
        zc_ref[...] = z


def _inproj(x, g, wz, tm=512, tn=512):
    n = x.shape[0]
    n_rw = ZR_COLS // tn
    n_cv = ZC_COLS // tn
    return pl.pallas_call(
        functools.partial(_inproj_kernel, n_rw),
        grid=(n // tm, n_rw + n_cv),
        in_specs=[
            pl.BlockSpec((tm, D_MODEL), lambda i, j: (i, 0)),
            pl.BlockSpec((1, D_MODEL), lambda i, j: (0, 0)),
            pl.BlockSpec((D_MODEL, tn), lambda i, j: (0, j)),
        ],
        out_specs=[
            pl.BlockSpec((tm, tn), lambda i, j: (i, jnp.minimum(j, n_rw - 1))),
            pl.BlockSpec((tm, tn), lambda i, j: (i, jnp.maximum(j - n_rw, 0))),
        ],
        out_shape=[
            jax.ShapeDtypeStruct((n, ZR_COLS), F32),
            jax.ShapeDtypeStruct((n, ZC_COLS), F32),
        ],
        scratch_shapes=[pltpu.VMEM((tm, D_MODEL), BF16)],
        compiler_params=_cparams(("parallel", "arbitrary")),
        name="inproj",
    )(x, g, wz)


def _prep_kernel(seq_len, tb,
                 z_ref, zp_ref, zn_ref, mup_ref, mun_ref, w2_ref, a2_ref, g2_ref,
                 w0_ref, a0_ref, kk_w_ref, ka_ref, rk_ref, e_ref,
                 r_ref, v_ref, kk_ref, lwf_ref, kmf_ref, bf_ref, lwb_ref, kmb_ref, bb_ref,
                 g_ref, bonus_ref):
    i = pl.program_id(0)
    t0 = (i * tb) % seq_len
    first = t0 == 0
    last = (t0 + tb) == seq_len
    rows = lax.broadcasted_iota(jnp.int32, (tb, 1), 0)

    def shifted(c0, c1):
        z = z_ref[:, c0:c1]
        prev_row = jnp.where(first, 0.0, zp_ref[HALO - 1:HALO, c0:c1])
        next_row = jnp.where(last, 0.0, zn_ref[0:1, c0:c1])
        zp = jnp.where(rows == 0, prev_row, pltpu.roll(z, 1, axis=0))
        zn = jnp.where(rows == tb - 1, next_row, pltpu.roll(z, tb - 1, axis=0))
        return z + mup_ref[:, c0:c1] * (zp - z) + mun_ref[:, c0:c1] * (zn - z)

    r = shifted(0, D_RWKV)
    k = shifted(D_RWKV, 2 * D_RWKV)
    v = shifted(2 * D_RWKV, 3 * D_RWKV)
    lora = shifted(3 * D_RWKV, ZR_COLS)

    xw = jnp.tanh(lora[:, 0:128]).astype(BF16)
    xa = lora[:, 128:256].astype(BF16)
    xg = jax.nn.sigmoid(lora[:, 256:512]).astype(BF16)
    dw = _dot(xw, w2_ref[...]) + w0_ref[...]
    da = _dot(xa, a2_ref[...]) + a0_ref[...]
    g_ref[...] = _dot(xg, g2_ref[...])

    kraw = k * kk_w_ref[...]
    ss = _seg_sum(kraw * kraw, e_ref)
    kk = kraw * lax.rsqrt(jnp.maximum(ss, L2_EPS * L2_EPS))
    rk = _seg_sum(r * k * rk_ref[...], e_ref)
    bonus_ref[...] = rk * v
    r_ref[...] = r
    v_ref[...] = v
    kk_ref[...] = kk

    neg_c = -math.exp(-0.5)
    ka = ka_ref[...]
    for d, (lw_ref, km_ref, b_ref) in enumerate(((lwf_ref, kmf_ref, bf_ref), (lwb_ref, kmb_ref, bb_ref))):
        sl = slice(d * D_RWKV, (d + 1) * D_RWKV)
        lw_ref[...] = neg_c * jax.nn.sigmoid(dw[:, sl])
        a = jax.nn.sigmoid(da[:, sl])
        km_ref[...] = k * (1.0 + (a - 1.0) * ka)
        b_ref[...] = kk * a


def _prep(zr, seq_len, p, tb=256):
    n = zr.shape[0]
    hb = tb // HALO
    nhb = n // HALO
    vec = lambda w: pl.BlockSpec((1, w), lambda i: (0, 0))
    full = lambda a: pl.BlockSpec(a.shape, lambda i: (0, 0))
    out_spec = pl.BlockSpec((tb, D_RWKV), lambda i: (i, 0))
    return pl.pallas_call(
        functools.partial(_prep_kernel, seq_len, tb),
        grid=(n // tb,),
        in_specs=[
            pl.BlockSpec((tb, ZR_COLS), lambda i: (i, 0)),
            pl.BlockSpec((HALO, ZR_COLS), lambda i: (jnp.maximum(i * hb - 1, 0), 0)),
            pl.BlockSpec((HALO, ZR_COLS), lambda i: (jnp.minimum((i + 1) * hb, nhb - 1), 0)),
            vec(ZR_COLS), vec(ZR_COLS),
            full(p["w2"]), full(p["a2"]), full(p["g2"]),
            vec(2 * D_RWKV), vec(2 * D_RWKV), vec(D_RWKV), vec(D_RWKV), vec(D_RWKV),
            full(p["e"]),
        ],
        out_specs=[out_spec] * 11,
        out_shape=[jax.ShapeDtypeStruct((n, D_RWKV), F32)] * 11,
        compiler_params=_cparams(("parallel",)),
        name="rwkv_prep",
    )(zr, zr, zr, p["mu_prev"], p["mu_next"], p["w2"], p["a2"], p["g2"],
      p["w0"], p["a0"], p["k_k"], p["k_a"], p["r_k"], p["e"])


def _wkv_masks(reverse):
    t = lax.broadcasted_iota(jnp.int32, (CHUNK, GROUP), 0)
    s = lax.broadcasted_iota(jnp.int32, (CHUNK, GROUP), 1) % CHUNK
    rr = lax.broadcasted_iota(jnp.int32, (GROUP, GROUP), 0)
    cc = lax.broadcasted_iota(jnp.int32, (GROUP, GROUP), 1)
    ti = lax.broadcasted_iota(jnp.int32, (CHUNK, CHUNK), 0)
    si = lax.broadcasted_iota(jnp.int32, (CHUNK, CHUNK), 1)
    return dict(
        strict=(s > t) if reverse else (s < t),
        incl=(s >= t) if reverse else (s <= t),
        eye=(s == t).astype(F32),
        bd=(rr // HEAD_DIM) == (cc // HEAD_DIM),
        eye_g=rr == cc,
        tri=((si >= ti) if reverse else (si <= ti)).astype(BF16),
    )


def _bd(x, m):
    return jnp.where(m["bd"], jnp.concatenate([x] * (GROUP // CHUNK), axis=0), jnp.zeros((), x.dtype))


def _wkv_chunk(refs, off, s_ref, y_ref, reverse, m):
    r_ref, v_ref, kk_ref, lw_ref, km_ref, b_ref = refs
    sl = pl.ds(off, CHUNK)
    lw = lw_ref[sl, :]
    hi, mid, lo = _split3(lw)
    tri = m["tri"]
    cum = _dot(tri, hi) + _dot(tri, mid) + _dot(tri, lo)
    tot = cum[0:1, :] if reverse else cum[CHUNK - 1:CHUNK, :]
    e_cum = jnp.exp(cum)
    e_neg = jnp.exp(-cum)
    e_prev = jnp.exp(cum - lw)
    e_rest = jnp.exp(tot - cum)
    e_tot = jnp.exp(tot)
    kk = kk_ref[sl, :]
    b = b_ref[sl, :]
    km = km_ref[sl, :]
    v_all = v_ref[sl, :]
    a_t = -kk * e_prev
    r_t = r_ref[sl, :] * e_cum
    b_h = b * e_neg
    k_h = km * e_neg
    b_r = b * e_rest
    k_r = km * e_rest

    for g in range(N_GROUPS):
        ls = slice(g * GROUP, (g + 1) * GROUP)
        at_g = a_t[:, ls]
        rt_g = r_t[:, ls]
        v_g = v_all[:, ls]
        at_b = at_g.astype(BF16)
        v_b = v_g.astype(BF16)
        lhs = jnp.concatenate([at_g, rt_g], axis=0).astype(BF16)
        gb = _dot_nt(lhs, _bd(b_h[:, ls].astype(BF16), m))
        gk = _dot_nt(lhs, _bd(k_h[:, ls].astype(BF16), m))
        a_ab = jnp.where(m["strict"], gb[:CHUNK], 0.0)
        a_rb = jnp.where(m["incl"], gb[CHUNK:], 0.0)
        a_ak = jnp.where(m["strict"], gk[:CHUNK], 0.0)
        a_rk = jnp.where(m["incl"], gk[CHUNK:], 0.0)

        x = a_ab
        t_inv = m["eye"] + a_ab
        xb = x.astype(BF16)
        x = _dot(xb, _bd(xb, m))
        n_lvl = int(math.log2(CHUNK)) - 2
        for _ in range(n_lvl):
            xb = x.astype(BF16)
            res = _dot(jnp.concatenate([xb, t_inv.astype(BF16)], axis=0), _bd(xb, m))
            x = res[:CHUNK]
            t_inv = t_inv + res[CHUNK:]
        t_inv = t_inv + _dot(t_inv.astype(BF16), _bd(x.astype(BF16), m))

        av = _dot(jnp.concatenate([a_ak, a_rk], axis=0).astype(BF16), _bd(v_b, m))
        akv = av[:CHUNK]
        y0 = av[CHUNK:]
        tb_ = t_inv.astype(BF16)
        u0 = _dot(tb_, _bd(akv.astype(BF16), m))
        w = _dot(tb_, _bd(at_b, m))
        arb = a_rb.astype(BF16)
        u0_b = u0.astype(BF16)
        w_b = w.astype(BF16)
        q = rt_g + _dot(arb, _bd(w_b, m))
        y1 = y0 + _dot(arb, _bd(u0_b, m))
        br_b = b_r[:, ls].astype(BF16)
        kr_b = k_r[:, ls].astype(BF16)
        m_full = _dot_tn(br_b, w_b)
        d_full = _dot_tn(jnp.concatenate([br_b, kr_b], axis=0),
                         jnp.concatenate([u0_b, v_b], axis=0))
        m_bd = jnp.where(m["bd"], m_full, 0.0) + jnp.where(m["eye_g"], e_tot[:, ls], 0.0)
        d_bd = jnp.where(m["bd"], d_full, 0.0)

        s0 = s_ref[g].astype(BF16)
        y_ref[sl, ls] = _dot(q.astype(BF16), s0) + y1
        s_ref[g] = _dot(m_bd.astype(BF16), s0) + d_bd


def _wkv_kernel(n_chunks, *refs):
    fwd_refs = refs[0:6]
    bwd_refs = refs[6:12]
    yf_ref, yb_ref, sf_ref, sb_ref = refs[12:16]

    @pl.when(pl.program_id(1) == 0)
    def _():
        sf_ref[...] = jnp.zeros_like(sf_ref)
        sb_ref[...] = jnp.zeros_like(sb_ref)

    mf = _wkv_masks(False)
    mb = _wkv_masks(True)

    def body(c, carry):
        _wkv_chunk(fwd_refs, pl.multiple_of(c * CHUNK, CHUNK), sf_ref, yf_ref, False, mf)
        _wkv_chunk(bwd_refs, pl.multiple_of((n_chunks - 1 - c) * CHUNK, CHUNK), sb_ref, yb_ref, True, mb)
        return carry

    lax.fori_loop(0, n_chunks, body, 0)


def _wkv(pp, n_seq, seq_len, tb=256):
    (r, v, kk, lwf, kmf, bf, lwb, kmb, bb) = pp
    n = r.shape[0]
    nb = seq_len // tb
    fspec = pl.BlockSpec((tb, D_RWKV), lambda s, i: (s * nb + i, 0))
    bspec = pl.BlockSpec((tb, D_RWKV), lambda s, i: (s * nb + nb - 1 - i, 0))
    return pl.pallas_call(
        functools.partial(_wkv_kernel, tb // CHUNK),
        grid=(n_seq, nb),
        in_specs=[fspec] * 6 + [bspec] * 6,
        out_specs=[fspec, bspec],
        out_shape=[jax.ShapeDtypeStruct((n, D_RWKV), F32)] * 2,
        scratch_shapes=[pltpu.VMEM((N_GROUPS, GROUP, GROUP), F32)] * 2,
        compiler_params=_cparams(("parallel", "arbitrary")),
        name="wkv",
    )(r, v, kk, lwf, kmf, bf, r, v, kk, lwb, kmb, bb)


def _post_kernel(yf_ref, yb_ref, g_ref, bonus_ref, gnw_ref, gnb_ref, e_ref, o_ref):
    y = yf_ref[...] + yb_ref[...]
    mu = _seg_sum(y, e_ref) * (1.0 / HEAD_DIM)
    d = y - mu
    var = _seg_sum(d * d, e_ref) * (1.0 / HEAD_DIM)
    yn = d * lax.rsqrt(var + GN_EPS) * gnw_ref[...] + gnb_ref[...]
    o_ref[...] = ((yn + bonus_ref[...]) * g_ref[...]).astype(o_ref.dtype)


def _post(yf, yb, g, bonus, p, tb=512):
    n = yf.shape[0]
    spec = pl.BlockSpec((tb, D_RWKV), lambda i: (i, 0))
    vec = pl.BlockSpec((1, D_RWKV), lambda i: (0, 0))
    return pl.pallas_call(
        _post_kernel,
        grid=(n // tb,),
        in_specs=[spec] * 4 + [vec, vec, pl.BlockSpec(p["e"].shape, lambda i: (0, 0))],
        out_specs=spec,
        out_shape=jax.ShapeDtypeStruct((n, D_RWKV), BF16),
        compiler_params=_cparams(("parallel",)),
        name="rwkv_post",
    )(yf, yb, g, bonus, p["gn_w"], p["gn_b"], p["e"])


CONV_ROWS = 64
LANES = 128


def _conv_kernel(seq_len, tb, zc_ref, zp_ref, zn_ref, dww_ref, dwb_ref, lnw_ref, lnb_ref,
                 o_ref, u_scr, c_scr):
    i = pl.program_id(0)
    t0 = (i * tb) % seq_len
    first = t0 == 0
    last = (t0 + tb) == seq_len

    def glu(ref):
        return ref[:, 0:D_CONV] * jax.nn.sigmoid(ref[:, D_CONV:2 * D_CONV])

    u_scr[HALO:HALO + tb, :] = glu(zc_ref)
    u_scr[0:HALO, :] = jnp.where(first, 0.0, glu(zp_ref))
    u_scr[HALO + tb:2 * HALO + tb, :] = jnp.where(last, 0.0, glu(zn_ref))

    base = HALO - CONV_HALF
    for cb in range(D_CONV // LANES):
        cs = slice(cb * LANES, (cb + 1) * LANES)
        wcol = dww_ref[:, cs]
        bias = dwb_ref[:, cs]
        for rc in range(tb // CONV_ROWS):
            r0 = rc * CONV_ROWS
            acc = jnp.zeros((CONV_ROWS, LANES), F32) + bias
            for j in range(CONV_WIDTH):
                acc = acc + u_scr[r0 + base + j:r0 + base + j + CONV_ROWS, cs] * wcol[j:j + 1, :]
            c_scr[r0:r0 + CONV_ROWS, cs] = acc

    c = c_scr[...]
    cm = jnp.mean(c, axis=-1, keepdims=True)
    d = c - cm
    cv = jnp.mean(d * d, axis=-1, keepdims=True)
    y = d * lax.rsqrt(cv + LN_EPS) * lnw_ref[...] + lnb_ref[...]
    o_ref[...] = (y * jax.nn.sigmoid(y)).astype(o_ref.dtype)


def _conv(zc, seq_len, p, tb=256):
    n = zc.shape[0]
    hb = tb // HALO
    nhb = n // HALO
    vec = pl.BlockSpec((1, D_CONV), lambda i: (0, 0))
    return pl.pallas_call(
        functools.partial(_conv_kernel, seq_len, tb),
        grid=(n // tb,),
        in_specs=[
            pl.BlockSpec((tb, ZC_COLS), lambda i: (i, 0)),
            pl.BlockSpec((HALO, ZC_COLS), lambda i: (jnp.maximum(i * hb - 1, 0), 0)),
            pl.BlockSpec((HALO, ZC_COLS), lambda i: (jnp.minimum((i + 1) * hb, nhb - 1), 0)),
            pl.BlockSpec((CONV_WIDTH, D_CONV), lambda i: (0, 0)),
            vec, vec, vec,
        ],
        out_specs=pl.BlockSpec((tb, D_CONV), lambda i: (i, 0)),
        out_shape=jax.ShapeDtypeStruct((n, D_CONV), BF16),
        scratch_shapes=[pltpu.VMEM((tb + 2 * HALO, D_CONV), F32), pltpu.VMEM((tb, D_CONV), F32)],
        compiler_params=_cparams(("parallel",)),
        name="conv_module",
    )(zc, zc, zc, p["dw_w"], p["dw_b"], p["cln_w"], p["cln_b"])


def _rms_rows(x, g):
    ms = jnp.mean(x * x, axis=-1, keepdims=True)
    return x * lax.rsqrt(ms + RMS_EPS) * g


def _outproj_kernel(orw_ref, ocv_ref, x_ref, w_ref, gpost_ref, gpre_ref, x1_ref, hm_ref):
    mix = _dot(orw_ref[...], w_ref[0:D_RWKV, :]) + _dot(ocv_ref[...], w_ref[D_RWKV:D_MODEL, :])
    x1 = x_ref[...] + _rms_rows(mix, gpost_ref[...])
    x1_ref[...] = x1
    hm_ref[...] = _rms_rows(x1, gpre_ref[...]).astype(hm_ref.dtype)


def _outproj(o_rwkv, o_conv, x, p, tm=512):
    n = x.shape[0]
    half = pl.BlockSpec((tm, D_RWKV), lambda i: (i, 0))
    row = pl.BlockSpec((tm, D_MODEL), lambda i: (i, 0))
    vec = pl.BlockSpec((1, D_MODEL), lambda i: (0, 0))
    return pl.pallas_call(
        _outproj_kernel,
        grid=(n // tm,),
        in_specs=[half, half, row, pl.BlockSpec((D_MODEL, D_MODEL), lambda i: (0, 0)), vec, vec],
        out_specs=[row, row],
        out_shape=[jax.ShapeDtypeStruct((n, D_MODEL), F32), jax.ShapeDtypeStruct((n, D_MODEL), BF16)],
        compiler_params=_cparams(("parallel",)),
        name="outproj",
    )(o_rwkv, o_conv, x, p["w_out"], p["g_post_mix"], p["g_pre_mlp"])


def _mlp_kernel(hm_ref, x1_ref, wup_ref, wdn_ref, g_ref, o_ref, acc_ref):
    j = pl.program_id(1)
    h = _dot(hm_ref[...], wup_ref[...])
    h = jnp.square(jnp.maximum(h, 0.0)).astype(BF16)
    part = _dot(h, wdn_ref[...])

    @pl.when(j == 0)
    def _():
        acc_ref[...] = part

    @pl.when(j > 0)
    def _():
        acc_ref[...] += part

    @pl.when(j == pl.num_programs(1) - 1)
    def _():
        o_ref[...] = x1_ref[...] + _rms_rows(acc_ref[...], g_ref[...])


def _mlp(hm, x1, p, tm=512, tf=1024):
    n = hm.shape[0]
    row = pl.BlockSpec((tm, D_MODEL), lambda i, j: (i, 0))
    return pl.pallas_call(
        _mlp_kernel,
        grid=(n // tm, D_FF // tf),
        in_specs=[
            row, row,
            pl.BlockSpec((D_MODEL, tf), lambda i, j: (0, j)),
            pl.BlockSpec((tf, D_MODEL), lambda i, j: (j, 0)),
            pl.BlockSpec((1, D_MODEL), lambda i, j: (0, 0)),
        ],
        out_specs=row,
        out_shape=jax.ShapeDtypeStruct((n, D_MODEL), F32),
        scratch_shapes=[pltpu.VMEM((tm, D_MODEL), F32)],
        compiler_params=_cparams(("parallel", "arbitrary")),
        name="mlp",
    )(hm, x1, p["w_up"], p["w_down"], p["g_post_mlp"])


def _pad_rows(w, rows, offset):
    out = jnp.zeros((rows, w.shape[1]), w.dtype)
    return out.at[offset:offset + w.shape[0]].set(w)


def _layer_params(l, g_pre_mix, w_in, mu_prev, mu_next, w0_f, w2_f, w0_b, w2_b, a0_f, a2_f, a0_b, a2_b,
                  g2, k_k, k_a, r_k, gn_w, gn_b, dw_w, dw_b, cln_w, cln_b, w_out, g_post_mix,
                  g_pre_mlp, w_up, w_down, g_post_mlp):
    rwkv_cols = 3 * D_RWKV + LORA_COLS
    row = lambda a: a[l].reshape(1, -1).astype(F32)
    pad_cols = lambda a, width: jnp.pad(a, ((0, 0), (0, width - a.shape[1])))
    w = w_in[l]
    wz = jnp.concatenate([pad_cols(w[:, :rwkv_cols], ZR_COLS), w[:, rwkv_cols:]], axis=1).astype(BF16)
    heads = jnp.arange(D_RWKV) // HEAD_DIM
    return dict(
        g_pre_mix=row(g_pre_mix), wz=wz,
        mu_prev=pad_cols(row(mu_prev), ZR_COLS), mu_next=pad_cols(row(mu_next), ZR_COLS),
        w2=jnp.concatenate([_pad_rows(w2_f[l], 128, 0), _pad_rows(w2_b[l], 128, DECAY_LORA)], axis=1).astype(BF16),
        a2=jnp.concatenate([_pad_rows(a2_f[l], 128, 0), _pad_rows(a2_b[l], 128, AAA_LORA)], axis=1).astype(BF16),
        g2=_pad_rows(g2[l], 256, 0).astype(BF16),
        w0=jnp.concatenate([row(w0_f), row(w0_b)], axis=1),
        a0=jnp.concatenate([row(a0_f), row(a0_b)], axis=1),
        k_k=row(k_k), k_a=row(k_a), r_k=row(r_k), gn_w=row(gn_w), gn_b=row(gn_b),
        e=(heads[:, None] == heads[None, :]).astype(BF16),
        dw_w=dw_w[l].astype(F32), dw_b=row(dw_b), cln_w=row(cln_w), cln_b=row(cln_b),
        w_out=w_out[l].astype(BF16), g_post_mix=row(g_post_mix), g_pre_mlp=row(g_pre_mlp),
        w_up=w_up[l].astype(BF16), w_down=w_down[l].astype(BF16), g_post_mlp=row(g_post_mlp),
    )


def _layer(x, n_seq, seq_len, p):
    zr, zc = _inproj(x, p["g_pre_mix"], p["wz"])
    prep = _prep(zr, seq_len, p)
    g, bonus = prep[9], prep[10]
    yf, yb = _wkv(prep[:9], n_seq, seq_len)
    o_rwkv = _post(yf, yb, g, bonus, p)
    o_conv = _conv(zc, seq_len, p)
    x1, hm = _outproj(o_rwkv, o_conv, x, p)
    return _mlp(hm, x1, p)


def _forward(x, weights):
    n_seq, seq_len, _ = x.shape
    h = x.reshape(n_seq * seq_len, D_MODEL)
    depth = weights[0].shape[0]
    for l in range(depth):
        h = _layer(h, n_seq, seq_len, _layer_params(l, *weights))
    return h.reshape(n_seq, seq_len, D_MODEL)


def kernel(x_prompt, x_sample, g_pre_mix, w_in, mu_prev, mu_next, w0_f, w2_f, w0_b, w2_b, a0_f, a2_f, a0_b, a2_b, g2, k_k, k_a, r_k, gn_w, gn_b, dw_w, dw_b, cln_w, cln_b, w_out, g_post_mix, g_pre_mlp, w_up, w_down, g_post_mlp):
    weights = (g_pre_mix, w_in, mu_prev, mu_next, w0_f, w2_f, w0_b, w2_b, a0_f, a2_f, a0_b, a2_b,
               g2, k_k, k_a, r_k, gn_w, gn_b, dw_w, dw_b, cln_w, cln_b, w_out, g_post_mix,
               g_pre_mlp, w_up, w_down, g_post_mlp)
    assert x_prompt.shape[1] == x_sample.shape[1]
    nb = x_prompt.shape[0]
    y = _forward(jnp.concatenate([x_prompt, x_sample], axis=0), weights)
    return (y[:nb], y[nb:])
```

```python
import functools
import math

import jax
import jax.numpy as jnp
from jax import lax
from jax.experimental import pallas as pl
from jax.experimental.pallas import tpu as pltpu

F32 = jnp.float32
BF16 = jnp.bfloat16

D_MODEL = 2048
D_RWKV = 1024
D_CONV = 1024
HEAD_DIM = 64
N_HEADS = 16
DECAY_LORA = 64
AAA_LORA = 64
GATE_LORA = 160
CONV_WIDTH = 31
CONV_HALF = CONV_WIDTH // 2
D_FF = 4 * D_MODEL
RMS_EPS = 1e-6
LN_EPS = 1e-5
GN_EPS = 64e-5
L2_EPS = 1e-12

LORA_COLS = 2 * DECAY_LORA + 2 * AAA_LORA + GATE_LORA
LORA_PAD = 512
ZR_COLS = 3 * D_RWKV + LORA_PAD
ZC_COLS = 2 * D_CONV

CHUNK = 64
GROUP = 256
N_GROUPS = D_RWKV // GROUP
HALO = 16
SUBLANES = 8
LANES = 128

VMEM_LIMIT = 56 * 1024 * 1024


def _cparams(sem):
    return pltpu.CompilerParams(dimension_semantics=sem, vmem_limit_bytes=VMEM_LIMIT)


def _dot(a, b):
    return jnp.dot(a, b, preferred_element_type=F32)


def _dot_nt(a, b):
    return lax.dot_general(a, b, (((1,), (1,)), ((), ())), preferred_element_type=F32)


def _dot_tn(a, b):
    return lax.dot_general(a, b, (((0,), (0,)), ((), ())), preferred_element_type=F32)


def _split3(x):
    hi = x.astype(BF16)
    r1 = x - hi.astype(F32)
    mid = r1.astype(BF16)
    lo = (r1 - mid.astype(F32)).astype(BF16)
    return hi, mid, lo


def _seg_sum(x, e_ref):
    return _dot(x.astype(BF16), e_ref[...])


def _rms_rows(x, g):
    ms = jnp.mean(x * x, axis=-1, keepdims=True)
    return x * lax.rsqrt(ms + RMS_EPS) * g


def _lo_spec(tm, width, n_lo):
    return pl.BlockSpec((tm, width), lambda i, *_: (jnp.minimum(i, n_lo - 1), 0))


def _hi_spec(tm, width, n_lo):
    return pl.BlockSpec((tm, width), lambda i, *_: (jnp.maximum(i - n_lo, 0), 0))


def _inproj_kernel(n_lo, n_rw_tiles, xp_ref, xs_ref, g_ref, w_ref, zr_ref, zc_ref, h_scr):
    i = pl.program_id(0)
    j = pl.program_id(1)

    def norm(x_ref):
        h_scr[...] = _rms_rows(x_ref[...], g_ref[...]).astype(BF16)

    pl.when((j == 0) & (i < n_lo))(lambda: norm(xp_ref))
    pl.when((j == 0) & (i >= n_lo))(lambda: norm(xs_ref))

    z = _dot(h_scr[...], w_ref[...]).astype(BF16)

    @pl.when(j < n_rw_tiles)
    def _():
        zr_ref[...] = z

    @pl.when(j >= n_rw_tiles)
    def _():
        zc_ref[...] = z


def _inproj(xp, xs, g, wz, tm=1024, tn=512):
    n_lo = xp.shape[0] // tm
    n = xp.shape[0] + xs.shape[0]
    n_rw = ZR_COLS // tn
    n_cv = ZC_COLS // tn
    return pl.pallas_call(
        functools.partial(_inproj_kernel, n_lo, n_rw),
        grid=(n // tm, n_rw + n_cv),
        in_specs=[
            _lo_spec(tm, D_MODEL, n_lo),
            _hi_spec(tm, D_MODEL, n_lo),
            pl.BlockSpec((1, D_MODEL), lambda i, j: (0, 0)),
            pl.BlockSpec((D_MODEL, tn), lambda i, j: (0, j)),
        ],
        out_specs=[
            pl.BlockSpec((tm, tn), lambda i, j: (i, jnp.minimum(j, n_rw - 1))),
            pl.BlockSpec((tm, tn), lambda i, j: (i, jnp.maximum(j - n_rw, 0))),
        ],
        out_shape=[
            jax.ShapeDtypeStruct((n, ZR_COLS), BF16),
            jax.ShapeDtypeStruct((n, ZC_COLS), BF16),
        ],
        scratch_shapes=[pltpu.VMEM((tm, D_MODEL), BF16)],
        compiler_params=_cparams(("parallel", "arbitrary")),
        name="inproj",
    )(xp, xs, g, wz)


def _prep_kernel(seq_len, tb,
                 z_ref, zp_ref, zn_ref, mup_ref, mun_ref, w2_ref, a2_ref, g2_ref,
                 w0_ref, a0_ref, kk_w_ref, ka_ref, rk_ref, e_ref,
                 r_ref, v_ref, kk_ref, lwf_ref, kmf_ref, bf_ref, lwb_ref, kmb_ref, bb_ref,
                 g_ref, bonus_ref):
    i = pl.program_id(0)
    t0 = (i * tb) % seq_len
    first = t0 == 0
    last = (t0 + tb) == seq_len
    rows = lax.broadcasted_iota(jnp.int32, (tb, 1), 0)

    def shifted(c0, c1):
        z = z_ref[:, c0:c1].astype(F32)
        prev_row = jnp.where(first, 0.0, zp_ref[HALO - 1:HALO, c0:c1].astype(F32))
        next_row = jnp.where(last, 0.0, zn_ref[0:1, c0:c1].astype(F32))
        zp = jnp.where(rows == 0, prev_row, pltpu.roll(z, 1, axis=0))
        zn = jnp.where(rows == tb - 1, next_row, pltpu.roll(z, tb - 1, axis=0))
        return z + mup_ref[:, c0:c1] * (zp - z) + mun_ref[:, c0:c1] * (zn - z)

    r = shifted(0, D_RWKV)
    k = shifted(D_RWKV, 2 * D_RWKV)
    v = shifted(2 * D_RWKV, 3 * D_RWKV)
    lora = shifted(3 * D_RWKV, ZR_COLS)

    xw = jnp.tanh(lora[:, 0:128]).astype(BF16)
    xa = lora[:, 128:256].astype(BF16)
    xg = jax.nn.sigmoid(lora[:, 256:512]).astype(BF16)
    dw = _dot(xw, w2_ref[...]) + w0_ref[...]
    da = _dot(xa, a2_ref[...]) + a0_ref[...]
    g_ref[...] = _dot(xg, g2_ref[...]).astype(g_ref.dtype)

    kraw = k * kk_w_ref[...]
    ss = _seg_sum(kraw * kraw, e_ref)
    kk = kraw * lax.rsqrt(jnp.maximum(ss, L2_EPS * L2_EPS))
    rk = _seg_sum(r * k * rk_ref[...], e_ref)
    bonus_ref[...] = (rk * v).astype(bonus_ref.dtype)
    r_ref[...] = r.astype(r_ref.dtype)
    v_ref[...] = v.astype(v_ref.dtype)
    kk_ref[...] = kk.astype(kk_ref.dtype)

    neg_c = -math.exp(-0.5)
    ka = ka_ref[...]
    for d, (lw_ref, km_ref, b_ref) in enumerate(((lwf_ref, kmf_ref, bf_ref), (lwb_ref, kmb_ref, bb_ref))):
        sl = slice(d * D_RWKV, (d + 1) * D_RWKV)
        lw_ref[...] = neg_c * jax.nn.sigmoid(dw[:, sl])
        a = jax.nn.sigmoid(da[:, sl])
        km_ref[...] = (k * (1.0 + (a - 1.0) * ka)).astype(km_ref.dtype)
        b_ref[...] = (kk * a).astype(b_ref.dtype)


_PREP_OUT_DTYPES = (BF16, BF16, BF16, F32, BF16, BF16, F32, BF16, BF16, BF16, BF16)


def _prep(zr, seq_len, p, tb=256):
    n = zr.shape[0]
    hb = tb // HALO
    nhb = n // HALO
    vec = lambda w: pl.BlockSpec((1, w), lambda i: (0, 0))
    full = lambda a: pl.BlockSpec(a.shape, lambda i: (0, 0))
    out_spec = pl.BlockSpec((tb, D_RWKV), lambda i: (i, 0))
    return pl.pallas_call(
        functools.partial(_prep_kernel, seq_len, tb),
        grid=(n // tb,),
        in_specs=[
            pl.BlockSpec((tb, ZR_COLS), lambda i: (i, 0)),
            pl.BlockSpec((HALO, ZR_COLS), lambda i: (jnp.maximum(i * hb - 1, 0), 0)),
            pl.BlockSpec((HALO, ZR_COLS), lambda i: (jnp.minimum((i + 1) * hb, nhb - 1), 0)),
            vec(ZR_COLS), vec(ZR_COLS),
            full(p["w2"]), full(p["a2"]), full(p["g2"]),
            vec(2 * D_RWKV), vec(2 * D_RWKV), vec(D_RWKV), vec(D_RWKV), vec(D_RWKV),
            full(p["e"]),
        ],
        out_specs=[out_spec] * 11,
        out_shape=[jax.ShapeDtypeStruct((n, D_RWKV), dt) for dt in _PREP_OUT_DTYPES],
        compiler_params=_cparams(("parallel",)),
        name="rwkv_prep",
    )(zr, zr, zr, p["mu_prev"], p["mu_next"], p["w2"], p["a2"], p["g2"],
      p["w0"], p["a0"], p["k_k"], p["k_a"], p["r_k"], p["e"])


def _wkv_masks(reverse):
    t = lax.broadcasted_iota(jnp.int32, (CHUNK, GROUP), 0)
    lane = lax.broadcasted_iota(jnp.int32, (CHUNK, GROUP), 1)
    s = lane % CHUNK
    rr = lax.broadcasted_iota(jnp.int32, (GROUP, GROUP), 0)
    cc = lax.broadcasted_iota(jnp.int32, (GROUP, GROUP), 1)
    ti = lax.broadcasted_iota(jnp.int32, (CHUNK, CHUNK), 0)
    si = lax.broadcasted_iota(jnp.int32, (CHUNK, CHUNK), 1)
    return dict(
        strict=(s > t) if reverse else (s < t),
        incl=(s >= t) if reverse else (s <= t),
        eye=(s == t).astype(F32),
        bd=(rr // HEAD_DIM) == (cc // HEAD_DIM),
        lane_head=lane // HEAD_DIM,
        tri=((si >= ti) if reverse else (si <= ti)).astype(BF16),
    )


def _bd(x, m):
    return jnp.where(m["bd"], jnp.concatenate([x] * (GROUP // CHUNK), axis=0), jnp.zeros((), x.dtype))


def _diag_blocks(full, m):
    out = full[0:CHUNK]
    for h in range(1, GROUP // CHUNK):
        out = jnp.where(m["lane_head"] == h, full[h * CHUNK:(h + 1) * CHUNK], out)
    return out


def _wkv_stage0(refs, off, reverse, m):
    r_ref, v_ref, kk_ref, lw_ref, km_ref, b_ref = refs
    sl = pl.ds(off, CHUNK)
    lw = lw_ref[sl, :]
    hi, mid, lo = _split3(lw)
    tri = m["tri"]
    cum = _dot(tri, hi) + _dot(tri, mid) + _dot(tri, lo)
    tot = cum[0:1, :] if reverse else cum[CHUNK - 1:CHUNK, :]
    e_neg = jnp.exp(-cum)
    e_rest = jnp.exp(tot - cum)
    b = b_ref[sl, :].astype(F32)
    km = km_ref[sl, :].astype(F32)
    return dict(
        a_t=-kk_ref[sl, :].astype(F32) * jnp.exp(cum - lw),
        r_t=r_ref[sl, :].astype(F32) * jnp.exp(cum),
        b_h=b * e_neg, k_h=km * e_neg, b_r=b * e_rest, k_r=km * e_rest,
        v=v_ref[sl, :], e_tot=jnp.exp(tot),
    )


def _wkv_chunks(insts):
    rng = range(len(insts))
    ms = [it["m"] for it in insts]
    ls = [slice(it["g"] * GROUP, (it["g"] + 1) * GROUP) for it in insts]
    pre = [it["pre"] for it in insts]
    rt = [pre[i]["r_t"][:, ls[i]] for i in rng]
    at_b = [pre[i]["a_t"][:, ls[i]].astype(BF16) for i in rng]
    v_b = [pre[i]["v"][:, ls[i]].astype(BF16) for i in rng]
    lhs = [jnp.concatenate([at_b[i], rt[i].astype(BF16)], axis=0) for i in rng]
    gb = [_dot_nt(lhs[i], _bd(pre[i]["b_h"][:, ls[i]].astype(BF16), ms[i])) for i in rng]
    gk = [_dot_nt(lhs[i], _bd(pre[i]["k_h"][:, ls[i]].astype(BF16), ms[i])) for i in rng]
    a_ab = [jnp.where(ms[i]["strict"], gb[i][:CHUNK], 0.0) for i in rng]
    a_rb = [jnp.where(ms[i]["incl"], gb[i][CHUNK:], 0.0).astype(BF16) for i in rng]
    a_k = [jnp.concatenate([jnp.where(ms[i]["strict"], gk[i][:CHUNK], 0.0),
                            jnp.where(ms[i]["incl"], gk[i][CHUNK:], 0.0)], axis=0).astype(BF16) for i in rng]
    av = [_dot(a_k[i], _bd(v_b[i], ms[i])) for i in rng]

    t_inv = [ms[i]["eye"] + a_ab[i] for i in rng]
    xb = [a.astype(BF16) for a in a_ab]
    x = [_dot(xb[i], _bd(xb[i], ms[i])) for i in rng]
    for _ in range(int(math.log2(CHUNK)) - 2):
        xb = [xi.astype(BF16) for xi in x]
        res = [_dot(jnp.concatenate([xb[i], t_inv[i].astype(BF16)], axis=0), _bd(xb[i], ms[i])) for i in rng]
        x = [r[:CHUNK] for r in res]
        t_inv = [t_inv[i] + res[i][CHUNK:] for i in rng]
    res = [_dot(t_inv[i].astype(BF16), _bd(x[i].astype(BF16), ms[i])) for i in rng]
    t_b = [(t_inv[i] + res[i]).astype(BF16) for i in rng]

    u0 = [_dot(t_b[i], _bd(av[i][:CHUNK].astype(BF16), ms[i])) for i in rng]
    w = [_dot(t_b[i], _bd(at_b[i], ms[i])) for i in rng]
    u0_b = [u.astype(BF16) for u in u0]
    w_b = [wi.astype(BF16) for wi in w]
    q = [rt[i] + _dot(a_rb[i], _bd(w_b[i], ms[i])) for i in rng]
    y1 = [av[i][CHUNK:] + _dot(a_rb[i], _bd(u0_b[i], ms[i])) for i in rng]

    br_b = [pre[i]["b_r"][:, ls[i]].astype(BF16) for i in rng]
    kr_b = [pre[i]["k_r"][:, ls[i]].astype(BF16) for i in rng]
    m_full = [_dot_tn(br_b[i], w_b[i]) for i in rng]
    d_full = [_dot_tn(jnp.concatenate([br_b[i], kr_b[i]], axis=0),
                      jnp.concatenate([u0_b[i], v_b[i]], axis=0)) for i in rng]
    m_lc = [_diag_blocks(m_full[i], ms[i]) + ms[i]["eye"] * pre[i]["e_tot"][:, ls[i]] for i in rng]
    d_lc = [_diag_blocks(d_full[i], ms[i]) for i in rng]

    for i in rng:
        it = insts[i]
        s_bd = _bd(it["s_ref"][it["g"]].astype(BF16), ms[i])
        res = _dot(jnp.concatenate([m_lc[i], q[i]], axis=0).astype(BF16), s_bd)
        it["s_ref"][it["g"]] = res[:CHUNK] + d_lc[i]
        it["y_ref"][it["sl"], ls[i]] = (res[CHUNK:] + y1[i]).astype(it["y_ref"].dtype)


def _wkv_kernel(n_chunks, *refs):
    fwd_refs = refs[0:6]
    bwd_refs = refs[6:12]
    yf_ref, yb_ref, sf_ref, sb_ref = refs[12:16]

    @pl.when(pl.program_id(1) == 0)
    def _():
        sf_ref[...] = jnp.zeros_like(sf_ref)
        sb_ref[...] = jnp.zeros_like(sb_ref)

    mf = _wkv_masks(False)
    mb = _wkv_masks(True)

    def body(c, carry):
        off_f = pl.multiple_of(c * CHUNK, CHUNK)
        off_b = pl.multiple_of((n_chunks - 1 - c) * CHUNK, CHUNK)
        pre_f = _wkv_stage0(fwd_refs, off_f, False, mf)
        pre_b = _wkv_stage0(bwd_refs, off_b, True, mb)
        insts = []
        for g in range(N_GROUPS):
            insts.append(dict(pre=pre_f, g=g, m=mf, s_ref=sf_ref, y_ref=yf_ref, sl=pl.ds(off_f, CHUNK)))
            insts.append(dict(pre=pre_b, g=g, m=mb, s_ref=sb_ref, y_ref=yb_ref, sl=pl.ds(off_b, CHUNK)))
        _wkv_chunks(insts)
        return carry

    lax.fori_loop(0, n_chunks, body, 0)


def _wkv(pp, n_seq, seq_len, tb=256):
    (r, v, kk, lwf, kmf, bf, lwb, kmb, bb) = pp
    n = r.shape[0]
    nb = seq_len // tb
    fspec = pl.BlockSpec((tb, D_RWKV), lambda s, i: (s * nb + i, 0))
    bspec = pl.BlockSpec((tb, D_RWKV), lambda s, i: (s * nb + nb - 1 - i, 0))
    return pl.pallas_call(
        functools.partial(_wkv_kernel, tb // CHUNK),
        grid=(n_seq, nb),
        in_specs=[fspec] * 6 + [bspec] * 6,
        out_specs=[fspec, bspec],
        out_shape=[jax.ShapeDtypeStruct((n, D_RWKV), BF16)] * 2,
        scratch_shapes=[pltpu.VMEM((N_GROUPS, CHUNK, GROUP), F32)] * 2,
        compiler_params=_cparams(("parallel", "arbitrary")),
        name="wkv",
    )(r, v, kk, lwf, kmf, bf, r, v, kk, lwb, kmb, bb)


def _post_kernel(yf_ref, yb_ref, g_ref, bonus_ref, gnw_ref, gnb_ref, e_ref, o_ref):
    y = yf_ref[...].astype(F32) + yb_ref[...].astype(F32)
    mu = _seg_sum(y, e_ref) * (1.0 / HEAD_DIM)
    d = y - mu
    var = _seg_sum(d * d, e_ref) * (1.0 / HEAD_DIM)
    yn = d * lax.rsqrt(var + GN_EPS) * gnw_ref[...] + gnb_ref[...]
    o_ref[...] = ((yn + bonus_ref[...].astype(F32)) * g_ref[...].astype(F32)).astype(o_ref.dtype)


def _post(yf, yb, g, bonus, p, tb=512):
    n = yf.shape[0]
    spec = pl.BlockSpec((tb, D_RWKV), lambda i: (i, 0))
    vec = pl.BlockSpec((1, D_RWKV), lambda i: (0, 0))
    return pl.pallas_call(
        _post_kernel,
        grid=(n // tb,),
        in_specs=[spec] * 4 + [vec, vec, pl.BlockSpec(p["e"].shape, lambda i: (0, 0))],
        out_specs=spec,
        out_shape=jax.ShapeDtypeStruct((n, D_RWKV), BF16),
        compiler_params=_cparams(("parallel",)),
        name="rwkv_post",
    )(yf, yb, g, bonus, p["gn_w"], p["gn_b"], p["e"])


CONV_ROWS = 64


def _conv_kernel(seq_len, tb, zc_ref, zp_ref, zn_ref, dww_ref, dwb_ref, lnw_ref, lnb_ref,
                 o_ref, u_scr, sh_scr, c_scr):
    i = pl.program_id(0)
    t0 = (i * tb) % seq_len
    first = t0 == 0
    last = (t0 + tb) == seq_len

    def glu(ref):
        return ref[:, 0:D_CONV].astype(F32) * jax.nn.sigmoid(ref[:, D_CONV:2 * D_CONV].astype(F32))

    u_scr[HALO:HALO + tb, :] = glu(zc_ref)
    u_scr[0:HALO, :] = jnp.where(first, 0.0, glu(zp_ref))
    u_scr[HALO + tb:2 * HALO + tb, :] = jnp.where(last, 0.0, glu(zn_ref))

    sh_rows = tb + 2 * HALO - SUBLANES
    for s in range(1, SUBLANES):
        sh_scr[s - 1] = u_scr[s:s + sh_rows, :]

    base = HALO - CONV_HALF
    for cb in range(D_CONV // LANES):
        cs = slice(cb * LANES, (cb + 1) * LANES)
        wcol = dww_ref[:, cs]
        bias = dwb_ref[:, cs]
        for rc in range(tb // CONV_ROWS):
            r0 = rc * CONV_ROWS
            acc = jnp.zeros((CONV_ROWS, LANES), F32) + bias
            for j in range(CONV_WIDTH):
                s = (base + j) % SUBLANES
                q = r0 + base + j - s
                if s == 0:
                    u = u_scr[q:q + CONV_ROWS, cs]
                else:
                    u = sh_scr[s - 1, q:q + CONV_ROWS, cs]
                acc = acc + u * wcol[j:j + 1, :]
            c_scr[r0:r0 + CONV_ROWS, cs] = acc

    c = c_scr[...]
    cm = jnp.mean(c, axis=-1, keepdims=True)
    d = c - cm
    cv = jnp.mean(d * d, axis=-1, keepdims=True)
    y = d * lax.rsqrt(cv + LN_EPS) * lnw_ref[...] + lnb_ref[...]
    o_ref[...] = (y * jax.nn.sigmoid(y)).astype(o_ref.dtype)


def _conv(zc, seq_len, p, tb=256):
    n = zc.shape[0]
    hb = tb // HALO
    nhb = n // HALO
    vec = pl.BlockSpec((1, D_CONV), lambda i: (0, 0))
    return pl.pallas_call(
        functools.partial(_conv_kernel, seq_len, tb),
        grid=(n // tb,),
        in_specs=[
            pl.BlockSpec((tb, ZC_COLS), lambda i: (i, 0)),
            pl.BlockSpec((HALO, ZC_COLS), lambda i: (jnp.maximum(i * hb - 1, 0), 0)),
            pl.BlockSpec((HALO, ZC_COLS), lambda i: (jnp.minimum((i + 1) * hb, nhb - 1), 0)),
            pl.BlockSpec((CONV_WIDTH, D_CONV), lambda i: (0, 0)),
            vec, vec, vec,
        ],
        out_specs=pl.BlockSpec((tb, D_CONV), lambda i: (i, 0)),
        out_shape=jax.ShapeDtypeStruct((n, D_CONV), BF16),
        scratch_shapes=[
            pltpu.VMEM((tb + 2 * HALO, D_CONV), F32),
            pltpu.VMEM((SUBLANES - 1, tb + 2 * HALO - SUBLANES, D_CONV), F32),
            pltpu.VMEM((tb, D_CONV), F32),
        ],
        compiler_params=_cparams(("parallel",)),
        name="conv_module",
    )(zc, zc, zc, p["dw_w"], p["dw_b"], p["cln_w"], p["cln_b"])


def _outproj_kernel(n_lo, orw_ref, ocv_ref, xp_ref, xs_ref, w_ref, gpost_ref, gpre_ref, x1_ref, hm_ref):
    mix = _dot(orw_ref[...], w_ref[0:D_RWKV, :]) + _dot(ocv_ref[...], w_ref[D_RWKV:D_MODEL, :])
    upd = _rms_rows(mix, gpost_ref[...])

    def finish(x_ref):
        x1 = x_ref[...] + upd
        x1_ref[...] = x1
        hm_ref[...] = _rms_rows(x1, gpre_ref[...]).astype(hm_ref.dtype)

    i = pl.program_id(0)
    pl.when(i < n_lo)(lambda: finish(xp_ref))
    pl.when(i >= n_lo)(lambda: finish(xs_ref))


def _outproj(o_rwkv, o_conv, xp, xs, p, tm=512):
    n = o_rwkv.shape[0]
    n_lo = xp.shape[0] // tm
    half = pl.BlockSpec((tm, D_RWKV), lambda i: (i, 0))
    row = pl.BlockSpec((tm, D_MODEL), lambda i: (i, 0))
    vec = pl.BlockSpec((1, D_MODEL), lambda i: (0, 0))
    return pl.pallas_call(
        functools.partial(_outproj_kernel, n_lo),
        grid=(n // tm,),
        in_specs=[half, half, _lo_spec(tm, D_MODEL, n_lo), _hi_spec(tm, D_MODEL, n_lo),
                  pl.BlockSpec((D_MODEL, D_MODEL), lambda i: (0, 0)), vec, vec],
        out_specs=[row, row],
        out_shape=[jax.ShapeDtypeStruct((n, D_MODEL), F32), jax.ShapeDtypeStruct((n, D_MODEL), BF16)],
        compiler_params=_cparams(("parallel",)),
        name="outproj",
    )(o_rwkv, o_conv, xp, xs, p["w_out"], p["g_post_mix"], p["g_pre_mlp"])


def _mlp_kernel(n_lo, hm_ref, x1_ref, wup_ref, wdn_ref, g_ref, op_ref, os_ref, acc_ref):
    i = pl.program_id(0)
    j = pl.program_id(1)
    h = _dot(hm_ref[...], wup_ref[...])
    h = jnp.square(jnp.maximum(h, 0.0)).astype(BF16)
    part = _dot(h, wdn_ref[...])

    @pl.when(j == 0)
    def _():
        acc_ref[...] = part

    @pl.when(j > 0)
    def _():
        acc_ref[...] += part

    def finish(o_ref):
        o_ref[...] = x1_ref[...] + _rms_rows(acc_ref[...], g_ref[...])

    is_last = j == pl.num_programs(1) - 1
    pl.when(is_last & (i < n_lo))(lambda: finish(op_ref))
    pl.when(is_last & (i >= n_lo))(lambda: finish(os_ref))


def _mlp(hm, x1, n_prompt, p, tm=512, tf=1024):
    n = hm.shape[0]
    n_lo = n_prompt // tm
    row = pl.BlockSpec((tm, D_MODEL), lambda i, j: (i, 0))
    return pl.pallas_call(
        functools.partial(_mlp_kernel, n_lo),
        grid=(n // tm, D_FF // tf),
        in_specs=[
            row, row,
            pl.BlockSpec((D_MODEL, tf), lambda i, j: (0, j)),
            pl.BlockSpec((tf, D_MODEL), lambda i, j: (j, 0)),
            pl.BlockSpec((1, D_MODEL), lambda i, j: (0, 0)),
        ],
        out_specs=[_lo_spec(tm, D_MODEL, n_lo), _hi_spec(tm, D_MODEL, n_lo)],
        out_shape=[jax.ShapeDtypeStruct((n_prompt, D_MODEL), F32),
                   jax.ShapeDtypeStruct((n - n_prompt, D_MODEL), F32)],
        scratch_shapes=[pltpu.VMEM((tm, D_MODEL), F32)],
        compiler_params=_cparams(("arbitrary", "arbitrary")),
        name="mlp",
    )(hm, x1, p["w_up"], p["w_down"], p["g_post_mlp"])


def _pad_rows(w, rows, offset):
    out = jnp.zeros((rows, w.shape[1]), w.dtype)
    return out.at[offset:offset + w.shape[0]].set(w)


def _layer_params(l, g_pre_mix, w_in, mu_prev, mu_next, w0_f, w2_f, w0_b, w2_b, a0_f, a2_f, a0_b, a2_b,
                  g2, k_k, k_a, r_k, gn_w, gn_b, dw_w, dw_b, cln_w, cln_b, w_out, g_post_mix,
                  g_pre_mlp, w_up, w_down, g_post_mlp):
    rwkv_cols = 3 * D_RWKV + LORA_COLS
    row = lambda a: a[l].reshape(1, -1).astype(F32)
    pad_cols = lambda a, width: jnp.pad(a, ((0, 0), (0, width - a.shape[1])))
    w = w_in[l].astype(BF16)
    wz = jnp.concatenate([pad_cols(w[:, :rwkv_cols], ZR_COLS), w[:, rwkv_cols:]], axis=1)
    heads = jnp.arange(D_RWKV) // HEAD_DIM
    return dict(
        g_pre_mix=row(g_pre_mix), wz=wz,
        mu_prev=pad_cols(row(mu_prev), ZR_COLS), mu_next=pad_cols(row(mu_next), ZR_COLS),
        w2=jnp.concatenate([_pad_rows(w2_f[l], 128, 0), _pad_rows(w2_b[l], 128, DECAY_LORA)], axis=1).astype(BF16),
        a2=jnp.concatenate([_pad_rows(a2_f[l], 128, 0), _pad_rows(a2_b[l], 128, AAA_LORA)], axis=1).astype(BF16),
        g2=_pad_rows(g2[l], 256, 0).astype(BF16),
        w0=jnp.concatenate([row(w0_f), row(w0_b)], axis=1),
        a0=jnp.concatenate([row(a0_f), row(a0_b)], axis=1),
        k_k=row(k_k), k_a=row(k_a), r_k=row(r_k), gn_w=row(gn_w), gn_b=row(gn_b),
        e=(heads[:, None] == heads[None, :]).astype(BF16),
        dw_w=dw_w[l].astype(F32), dw_b=row(dw_b), cln_w=row(cln_w), cln_b=row(cln_b),
        w_out=w_out[l].astype(BF16), g_post_mix=row(g_post_mix), g_pre_mlp=row(g_pre_mlp),
        w_up=w_up[l].astype(BF16), w_down=w_down[l].astype(BF16), g_post_mlp=row(g_post_mlp),
    )


def _layer(xp, xs, seq_len, p):
    n_seq = (xp.shape[0] + xs.shape[0]) // seq_len
    zr, zc = _inproj(xp, xs, p["g_pre_mix"], p["wz"])
    prep = _prep(zr, seq_len, p)
    g, bonus = prep[9], prep[10]
    yf, yb = _wkv(prep[:9], n_seq, seq_len)
    o_rwkv = _post(yf, yb, g, bonus, p)
    o_conv = _conv(zc, seq_len, p)
    x1, hm = _outproj(o_rwkv, o_conv, xp, xs, p)
    return _mlp(hm, x1, xp.shape[0], p)


def _forward(x_prompt, x_sample, weights):
    seq_len = x_prompt.shape[1]
    assert x_sample.shape[1] == seq_len
    xp = x_prompt.reshape(-1, D_MODEL)
    xs = x_sample.reshape(-1, D_MODEL)
    depth = weights[0].shape[0]
    for l in range(depth):
        xp, xs = _layer(xp, xs, seq_len, _layer_params(l, *weights))
    return xp.reshape(x_prompt.shape), xs.reshape(x_sample.shape)


def kernel(x_prompt, x_sample, g_pre_mix, w_in, mu_prev, mu_next, w0_f, w2_f, w0_b, w2_b, a0_f, a2_f, a0_b, a2_b, g2, k_k, k_a, r_k, gn_w, gn_b, dw_w, dw_b, cln_w, cln_b, w_out, g_post_mix, g_pre_mlp, w_up, w_down, g_post_mlp):
    weights = (g_pre_mix, w_in, mu_prev, mu_next, w0_f, w2_f, w0_b, w2_b, a0_f, a2_f, a0_b, a2_b,
               g2, k_k, k_a, r_k, gn_w, gn_b, dw_w, dw_b, cln_w, cln_b, w_out, g_post_mix,
               g_pre_mlp, w_up, w_down, g_post_mlp)
    return _forward(x_prompt, x_sample, weights)
```

```python
import functools
import math

import jax
import jax.numpy as jnp
from jax import lax
from jax.experimental import pallas as pl
from jax.experimental.pallas import tpu as pltpu

F32 = jnp.float32
BF16 = jnp.bfloat16

D_MODEL = 2048
D_RWKV = 1024
D_CONV = 1024
HEAD_DIM = 64
N_HEADS = 16
DECAY_LORA = 64
AAA_LORA = 64
GATE_LORA = 160
CONV_WIDTH = 31
CONV_HALF = CONV_WIDTH // 2
D_FF = 4 * D_MODEL
RMS_EPS = 1e-6
LN_EPS = 1e-5
GN_EPS = 64e-5
L2_EPS = 1e-12

LORA_COLS = 2 * DECAY_LORA + 2 * AAA_LORA + GATE_LORA
LORA_PAD = 512
ZR_COLS = 3 * D_RWKV + LORA_PAD
ZC_COLS = 2 * D_CONV

CHUNK = 64
GROUP = 256
N_GROUPS = D_RWKV // GROUP
WAVE_LAG = 3
HALO = 16
SUBLANES = 8
LANES = 128

VMEM_LIMIT = 56 * 1024 * 1024


def _cparams(sem):
    return pltpu.CompilerParams(dimension_semantics=sem, vmem_limit_bytes=VMEM_LIMIT)


def _dot(a, b):
    return jnp.dot(a, b, preferred_element_type=F32)


def _dot_nt(a, b):
    return lax.dot_general(a, b, (((1,), (1,)), ((), ())), preferred_element_type=F32)


def _dot_tn(a, b):
    return lax.dot_general(a, b, (((0,), (0,)), ((), ())), preferred_element_type=F32)


def _split2(x):
    hi = x.astype(BF16)
    lo = (x - hi.astype(F32)).astype(BF16)
    return hi, lo


def _seg_sum(x, e_ref):
    return _dot(x.astype(BF16), e_ref[...])


def _rms_rows(x, g):
    ms = jnp.mean(x * x, axis=-1, keepdims=True)
    return x * lax.rsqrt(ms + RMS_EPS) * g


def _lo_spec(tm, width, n_lo):
    return pl.BlockSpec((tm, width), lambda i, *_: (jnp.minimum(i, n_lo - 1), 0))


def _hi_spec(tm, width, n_lo):
    return pl.BlockSpec((tm, width), lambda i, *_: (jnp.maximum(i - n_lo, 0), 0))


def _inproj_kernel(n_lo, xp_ref, xs_ref, g_ref, w_ref, z_ref, h_scr):
    i = pl.program_id(0)
    j = pl.program_id(1)

    def norm(x_ref):
        h_scr[...] = _rms_rows(x_ref[...], g_ref[...]).astype(BF16)

    pl.when((j == 0) & (i < n_lo))(lambda: norm(xp_ref))
    pl.when((j == 0) & (i >= n_lo))(lambda: norm(xs_ref))

    z_ref[...] = _dot(h_scr[...], w_ref[...]).astype(z_ref.dtype)


def _inproj(xp, xs, g, w, name, tm=1024, tn=512):
    n_lo = xp.shape[0] // tm
    n = xp.shape[0] + xs.shape[0]
    cols = w.shape[1]
    return pl.pallas_call(
        functools.partial(_inproj_kernel, n_lo),
        grid=(n // tm, cols // tn),
        in_specs=[
            _lo_spec(tm, D_MODEL, n_lo),
            _hi_spec(tm, D_MODEL, n_lo),
            pl.BlockSpec((1, D_MODEL), lambda i, j: (0, 0)),
            pl.BlockSpec((D_MODEL, tn), lambda i, j: (0, j)),
        ],
        out_specs=pl.BlockSpec((tm, tn), lambda i, j: (i, j)),
        out_shape=jax.ShapeDtypeStruct((n, cols), BF16),
        scratch_shapes=[pltpu.VMEM((tm, D_MODEL), BF16)],
        compiler_params=_cparams(("parallel", "arbitrary")),
        name=name,
    )(xp, xs, g, w)


def _prep_kernel(seq_len, tb,
                 z_ref, zp_ref, zn_ref, mup_ref, mun_ref, w2_ref, a2_ref, g2_ref,
                 w0_ref, a0_ref, kk_w_ref, ka_ref, rk_ref, e_ref,
                 r_ref, v_ref, kk_ref, lwf_ref, kmf_ref, bf_ref, lwb_ref, kmb_ref, bb_ref,
                 g_ref, bonus_ref):
    i = pl.program_id(0)
    t0 = (i * tb) % seq_len
    first = t0 == 0
    last = (t0 + tb) == seq_len
    rows = lax.broadcasted_iota(jnp.int32, (tb, 1), 0)

    def shifted(c0, c1):
        z = z_ref[:, c0:c1].astype(F32)
        prev_row = jnp.where(first, 0.0, zp_ref[HALO - 1:HALO, c0:c1].astype(F32))
        next_row = jnp.where(last, 0.0, zn_ref[0:1, c0:c1].astype(F32))
        zp = jnp.where(rows == 0, prev_row, pltpu.roll(z, 1, axis=0))
        zn = jnp.where(rows == tb - 1, next_row, pltpu.roll(z, tb - 1, axis=0))
        return z + mup_ref[:, c0:c1] * (zp - z) + mun_ref[:, c0:c1] * (zn - z)

    r = shifted(0, D_RWKV)
    k = shifted(D_RWKV, 2 * D_RWKV)
    v = shifted(2 * D_RWKV, 3 * D_RWKV)
    lora = shifted(3 * D_RWKV, ZR_COLS)

    xw = jnp.tanh(lora[:, 0:128]).astype(BF16)
    xa = lora[:, 128:256].astype(BF16)
    xg = jax.nn.sigmoid(lora[:, 256:512]).astype(BF16)
    dw = _dot(xw, w2_ref[...]) + w0_ref[...]
    da = _dot(xa, a2_ref[...]) + a0_ref[...]
    g_ref[...] = _dot(xg, g2_ref[...]).astype(g_ref.dtype)

    kraw = k * kk_w_ref[...]
    ss = _seg_sum(kraw * kraw, e_ref)
    kk = kraw * lax.rsqrt(jnp.maximum(ss, L2_EPS * L2_EPS))
    rk = _seg_sum(r * k * rk_ref[...], e_ref)
    bonus_ref[...] = (rk * v).astype(bonus_ref.dtype)
    r_ref[...] = r.astype(r_ref.dtype)
    v_ref[...] = v.astype(v_ref.dtype)
    kk_ref[...] = kk.astype(kk_ref.dtype)

    neg_c = -math.exp(-0.5)
    ka = ka_ref[...]
    for d, (lw_ref, km_ref, b_ref) in enumerate(((lwf_ref, kmf_ref, bf_ref), (lwb_ref, kmb_ref, bb_ref))):
        sl = slice(d * D_RWKV, (d + 1) * D_RWKV)
        lw_ref[...] = neg_c * jax.nn.sigmoid(dw[:, sl])
        a = jax.nn.sigmoid(da[:, sl])
        km_ref[...] = (k * (1.0 + (a - 1.0) * ka)).astype(km_ref.dtype)
        b_ref[...] = (kk * a).astype(b_ref.dtype)


_PREP_OUT_DTYPES = (BF16, BF16, BF16, F32, BF16, BF16, F32, BF16, BF16, BF16, BF16)


def _prep(zr, seq_len, p, tb=256):
    n = zr.shape[0]
    hb = tb // HALO
    nhb = n // HALO
    vec = lambda w: pl.BlockSpec((1, w), lambda i: (0, 0))
    full = lambda a: pl.BlockSpec(a.shape, lambda i: (0, 0))
    out_spec = pl.BlockSpec((tb, D_RWKV), lambda i: (i, 0))
    return pl.pallas_call(
        functools.partial(_prep_kernel, seq_len, tb),
        grid=(n // tb,),
        in_specs=[
            pl.BlockSpec((tb, ZR_COLS), lambda i: (i, 0)),
            pl.BlockSpec((HALO, ZR_COLS), lambda i: (jnp.maximum(i * hb - 1, 0), 0)),
            pl.BlockSpec((HALO, ZR_COLS), lambda i: (jnp.minimum((i + 1) * hb, nhb - 1), 0)),
            vec(ZR_COLS), vec(ZR_COLS),
            full(p["w2"]), full(p["a2"]), full(p["g2"]),
            vec(2 * D_RWKV), vec(2 * D_RWKV), vec(D_RWKV), vec(D_RWKV), vec(D_RWKV),
            full(p["e"]),
        ],
        out_specs=[out_spec] * 11,
        out_shape=[jax.ShapeDtypeStruct((n, D_RWKV), dt) for dt in _PREP_OUT_DTYPES],
        compiler_params=_cparams(("parallel",)),
        name="rwkv_prep",
    )(zr, zr, zr, p["mu_prev"], p["mu_next"], p["w2"], p["a2"], p["g2"],
      p["w0"], p["a0"], p["k_k"], p["k_a"], p["r_k"], p["e"])


def _wkv_masks(reverse):
    t = lax.broadcasted_iota(jnp.int32, (CHUNK, GROUP), 0)
    lane = lax.broadcasted_iota(jnp.int32, (CHUNK, GROUP), 1)
    s = lane % CHUNK
    rr = lax.broadcasted_iota(jnp.int32, (LANES, LANES), 0)
    cc = lax.broadcasted_iota(jnp.int32, (LANES, LANES), 1)
    ti = lax.broadcasted_iota(jnp.int32, (CHUNK, CHUNK), 0)
    si = lax.broadcasted_iota(jnp.int32, (CHUNK, CHUNK), 1)
    return dict(
        strict=(s > t) if reverse else (s < t),
        incl=(s >= t) if reverse else (s <= t),
        eye=(s == t).astype(F32),
        bd=(rr // HEAD_DIM) == (cc // HEAD_DIM),
        lane_head=lane // HEAD_DIM,
        tri=((si >= ti) if reverse else (si <= ti)).astype(BF16),
    )


def _bd(x, m):
    zero = jnp.zeros((), x.dtype)
    quad = lambda h: jnp.where(m["bd"], jnp.concatenate([x[:, h * LANES:(h + 1) * LANES]] * 2, axis=0), zero)
    zq = jnp.zeros((LANES, LANES), x.dtype)
    return jnp.concatenate([jnp.concatenate([quad(0), zq], axis=0),
                            jnp.concatenate([zq, quad(1)], axis=0)], axis=1)


def _diag_blocks(full, m):
    out = full[0:CHUNK]
    for h in range(1, GROUP // CHUNK):
        out = jnp.where(m["lane_head"] == h, full[h * CHUNK:(h + 1) * CHUNK], out)
    return out


def _wkv_stage0(refs, off, reverse, m):
    r_ref, v_ref, kk_ref, lw_ref, km_ref, b_ref = refs
    sl = pl.ds(off, CHUNK)
    lw = lw_ref[sl, :]
    hi, lo = _split2(lw)
    tri = m["tri"]
    cum = _dot(tri, hi) + _dot(tri, lo)
    tot = cum[0:1, :] if reverse else cum[CHUNK - 1:CHUNK, :]
    e_neg = jnp.exp(-cum)
    e_rest = jnp.exp(tot - cum)
    b = b_ref[sl, :].astype(F32)
    km = km_ref[sl, :].astype(F32)
    return dict(
        a_t=-kk_ref[sl, :].astype(F32) * jnp.exp(cum - lw),
        r_t=r_ref[sl, :].astype(F32) * jnp.exp(cum),
        b_h=b * e_neg, k_h=km * e_neg, b_r=b * e_rest, k_r=km * e_rest,
        v=v_ref[sl, :], e_tot=jnp.exp(tot),
    )


def _wkv_wave(fwd_refs, bwd_refs, off_f, off_b, mf, mb, yf_ref, yb_ref, sf_ref, sb_ref):
    pre_f = _wkv_stage0(fwd_refs, off_f, False, mf)
    pre_b = _wkv_stage0(bwd_refs, off_b, True, mb)
    yield
    insts = []
    for g in range(N_GROUPS):
        insts.append(dict(pre=pre_f, g=g, m=mf, s_ref=sf_ref, y_ref=yf_ref, sl=pl.ds(off_f, CHUNK)))
        insts.append(dict(pre=pre_b, g=g, m=mb, s_ref=sb_ref, y_ref=yb_ref, sl=pl.ds(off_b, CHUNK)))
    rng = range(len(insts))
    ms = [it["m"] for it in insts]
    ls = [slice(it["g"] * GROUP, (it["g"] + 1) * GROUP) for it in insts]
    pre = [it["pre"] for it in insts]
    rt = [pre[i]["r_t"][:, ls[i]] for i in rng]
    at_b = [pre[i]["a_t"][:, ls[i]].astype(BF16) for i in rng]
    v_b = [pre[i]["v"][:, ls[i]].astype(BF16) for i in rng]
    lhs = [jnp.concatenate([at_b[i], rt[i].astype(BF16)], axis=0) for i in rng]
    gb = [_dot_nt(lhs[i], _bd(pre[i]["b_h"][:, ls[i]].astype(BF16), ms[i])) for i in rng]
    yield
    gk = [_dot_nt(lhs[i], _bd(pre[i]["k_h"][:, ls[i]].astype(BF16), ms[i])) for i in rng]
    a_ab = [jnp.where(ms[i]["strict"], gb[i][:CHUNK], 0.0) for i in rng]
    a_rb = [jnp.where(ms[i]["incl"], gb[i][CHUNK:], 0.0).astype(BF16) for i in rng]
    yield
    t_inv = [ms[i]["eye"] + a_ab[i] for i in rng]
    xb = [a.astype(BF16) for a in a_ab]
    x = [_dot(xb[i], _bd(xb[i], ms[i])) for i in rng]
    a_k = [jnp.concatenate([jnp.where(ms[i]["strict"], gk[i][:CHUNK], 0.0),
                            jnp.where(ms[i]["incl"], gk[i][CHUNK:], 0.0)], axis=0).astype(BF16) for i in rng]
    av = [_dot(a_k[i], _bd(v_b[i], ms[i])) for i in rng]
    yield
    for _ in range(int(math.log2(CHUNK)) - 2):
        xb = [xi.astype(BF16) for xi in x]
        res = [_dot(jnp.concatenate([xb[i], t_inv[i].astype(BF16)], axis=0), _bd(xb[i], ms[i])) for i in rng]
        x = [r[:CHUNK] for r in res]
        t_inv = [t_inv[i] + res[i][CHUNK:] for i in rng]
        yield
    res = [_dot(t_inv[i].astype(BF16), _bd(x[i].astype(BF16), ms[i])) for i in rng]
    t_b = [(t_inv[i] + res[i]).astype(BF16) for i in rng]
    yield
    u0 = [_dot(t_b[i], _bd(av[i][:CHUNK].astype(BF16), ms[i])) for i in rng]
    w = [_dot(t_b[i], _bd(at_b[i], ms[i])) for i in rng]
    yield
    q = [rt[i] + _dot(a_rb[i], _bd(w[i].astype(BF16), ms[i])) for i in rng]
    y1 = [av[i][CHUNK:] + _dot(a_rb[i], _bd(u0[i].astype(BF16), ms[i])) for i in rng]
    yield
    bk_b = [jnp.concatenate([pre[i]["b_r"][:, ls[i]], pre[i]["k_r"][:, ls[i]]], axis=0).astype(BF16) for i in rng]
    wqd = [jnp.concatenate([w[i], q[i], ms[i]["eye"] * pre[i]["e_tot"][:, ls[i]]], axis=0).astype(BF16)
           for i in rng]
    res = [_dot(wqd[i], _bd(insts[i]["s_ref"][insts[i]["g"]].astype(BF16), ms[i])) for i in rng]
    for i in rng:
        it = insts[i]
        it["y_ref"][it["sl"], ls[i]] = (res[i][CHUNK:2 * CHUNK] + y1[i]).astype(it["y_ref"].dtype)
    yield
    upd = [_dot_tn(bk_b[i], jnp.concatenate([(res[i][:CHUNK] + u0[i]).astype(BF16), v_b[i]], axis=0))
           for i in rng]
    for i in rng:
        insts[i]["s_ref"][insts[i]["g"]] = res[i][2 * CHUNK:] + _diag_blocks(upd[i], ms[i])


def _interleave(gens, lag):
    pending = list(gens)
    active = []
    tick = 0
    while pending or active:
        if pending and tick % lag == 0:
            active.append(pending.pop(0))
        for g in list(active):
            if next(g, StopIteration) is StopIteration:
                active.remove(g)
        tick += 1


def _wkv_kernel(n_chunks, *refs):
    fwd_refs = refs[0:6]
    bwd_refs = refs[6:12]
    yf_ref, yb_ref, sf_ref, sb_ref = refs[12:16]

    @pl.when(pl.program_id(1) == 0)
    def _():
        sf_ref[...] = jnp.zeros_like(sf_ref)
        sb_ref[...] = jnp.zeros_like(sb_ref)

    mf = _wkv_masks(False)
    mb = _wkv_masks(True)
    waves = [_wkv_wave(fwd_refs, bwd_refs, c * CHUNK, (n_chunks - 1 - c) * CHUNK, mf, mb,
                       yf_ref, yb_ref, sf_ref, sb_ref) for c in range(n_chunks)]
    _interleave(waves, WAVE_LAG)


def _wkv(pp, n_seq, seq_len, tb=256):
    (r, v, kk, lwf, kmf, bf, lwb, kmb, bb) = pp
    n = r.shape[0]
    nb = seq_len // tb
    fspec = pl.BlockSpec((tb, D_RWKV), lambda s, i: (s * nb + i, 0))
    bspec = pl.BlockSpec((tb, D_RWKV), lambda s, i: (s * nb + nb - 1 - i, 0))
    return pl.pallas_call(
        functools.partial(_wkv_kernel, tb // CHUNK),
        grid=(n_seq, nb),
        in_specs=[fspec] * 6 + [bspec] * 6,
        out_specs=[fspec, bspec],
        out_shape=[jax.ShapeDtypeStruct((n, D_RWKV), BF16)] * 2,
        scratch_shapes=[pltpu.VMEM((N_GROUPS, CHUNK, GROUP), F32)] * 2,
        compiler_params=_cparams(("parallel", "arbitrary")),
        name="wkv",
    )(r, v, kk, lwf, kmf, bf, r, v, kk, lwb, kmb, bb)


def _post_kernel(yf_ref, yb_ref, g_ref, bonus_ref, gnw_ref, gnb_ref, e_ref, o_ref):
    y = yf_ref[...].astype(F32) + yb_ref[...].astype(F32)
    mu = _seg_sum(y, e_ref) * (1.0 / HEAD_DIM)
    d = y - mu
    var = _seg_sum(d * d, e_ref) * (1.0 / HEAD_DIM)
    yn = d * lax.rsqrt(var + GN_EPS) * gnw_ref[...] + gnb_ref[...]
    o_ref[...] = ((yn + bonus_ref[...].astype(F32)) * g_ref[...].astype(F32)).astype(o_ref.dtype)


def _post(yf, yb, g, bonus, p, tb=512):
    n = yf.shape[0]
    spec = pl.BlockSpec((tb, D_RWKV), lambda i: (i, 0))
    vec = pl.BlockSpec((1, D_RWKV), lambda i: (0, 0))
    return pl.pallas_call(
        _post_kernel,
        grid=(n // tb,),
        in_specs=[spec] * 4 + [vec, vec, pl.BlockSpec(p["e"].shape, lambda i: (0, 0))],
        out_specs=spec,
        out_shape=jax.ShapeDtypeStruct((n, D_RWKV), BF16),
        compiler_params=_cparams(("parallel",)),
        name="rwkv_post",
    )(yf, yb, g, bonus, p["gn_w"], p["gn_b"], p["e"])


CONV_ROWS = 64


def _conv_kernel(seq_len, tb, zc_ref, zp_ref, zn_ref, dww_ref, dwb_ref, lnw_ref, lnb_ref,
                 o_ref, u_scr, sh_scr, c_scr):
    i = pl.program_id(0)
    t0 = (i * tb) % seq_len
    first = t0 == 0
    last = (t0 + tb) == seq_len

    def glu(ref):
        return ref[:, 0:D_CONV].astype(F32) * jax.nn.sigmoid(ref[:, D_CONV:2 * D_CONV].astype(F32))

    u_scr[HALO:HALO + tb, :] = glu(zc_ref)
    u_scr[0:HALO, :] = jnp.where(first, 0.0, glu(zp_ref))
    u_scr[HALO + tb:2 * HALO + tb, :] = jnp.where(last, 0.0, glu(zn_ref))

    sh_rows = tb + 2 * HALO - SUBLANES
    for s in range(1, SUBLANES):
        sh_scr[s - 1] = u_scr[s:s + sh_rows, :]

    base = HALO - CONV_HALF
    for cb in range(D_CONV // LANES):
        cs = slice(cb * LANES, (cb + 1) * LANES)
        wcol = dww_ref[:, cs]
        bias = dwb_ref[:, cs]
        for rc in range(tb // CONV_ROWS):
            r0 = rc * CONV_ROWS
            acc = jnp.zeros((CONV_ROWS, LANES), F32) + bias
            for j in range(CONV_WIDTH):
                s = (base + j) % SUBLANES
                q = r0 + base + j - s
                if s == 0:
                    u = u_scr[q:q + CONV_ROWS, cs]
                else:
                    u = sh_scr[s - 1, q:q + CONV_ROWS, cs]
                acc = acc + u * wcol[j:j + 1, :]
            c_scr[r0:r0 + CONV_ROWS, cs] = acc

    c = c_scr[...]
    cm = jnp.mean(c, axis=-1, keepdims=True)
    d = c - cm
    cv = jnp.mean(d * d, axis=-1, keepdims=True)
    y = d * lax.rsqrt(cv + LN_EPS) * lnw_ref[...] + lnb_ref[...]
    o_ref[...] = (y * jax.nn.sigmoid(y)).astype(o_ref.dtype)


def _conv(zc, seq_len, p, tb=256):
    n = zc.shape[0]
    hb = tb // HALO
    nhb = n // HALO
    vec = pl.BlockSpec((1, D_CONV), lambda i: (0, 0))
    return pl.pallas_call(
        functools.partial(_conv_kernel, seq_len, tb),
        grid=(n // tb,),
        in_specs=[
            pl.BlockSpec((tb, ZC_COLS), lambda i: (i, 0)),
            pl.BlockSpec((HALO, ZC_COLS), lambda i: (jnp.maximum(i * hb - 1, 0), 0)),
            pl.BlockSpec((HALO, ZC_COLS), lambda i: (jnp.minimum((i + 1) * hb, nhb - 1), 0)),
            pl.BlockSpec((CONV_WIDTH, D_CONV), lambda i: (0, 0)),
            vec, vec, vec,
        ],
        out_specs=pl.BlockSpec((tb, D_CONV), lambda i: (i, 0)),
        out_shape=jax.ShapeDtypeStruct((n, D_CONV), BF16),
        scratch_shapes=[
            pltpu.VMEM((tb + 2 * HALO, D_CONV), F32),
            pltpu.VMEM((SUBLANES - 1, tb + 2 * HALO - SUBLANES, D_CONV), F32),
            pltpu.VMEM((tb, D_CONV), F32),
        ],
        compiler_params=_cparams(("parallel",)),
        name="conv_module",
    )(zc, zc, zc, p["dw_w"], p["dw_b"], p["cln_w"], p["cln_b"])


def _outproj_kernel(n_lo, orw_ref, ocv_ref, xp_ref, xs_ref, w_ref, gpost_ref, gpre_ref, x1_ref, hm_ref):
    mix = _dot(orw_ref[...], w_ref[0:D_RWKV, :]) + _dot(ocv_ref[...], w_ref[D_RWKV:D_MODEL, :])
    upd = _rms_rows(mix, gpost_ref[...])

    def finish(x_ref):
        x1 = x_ref[...] + upd
        x1_ref[...] = x1
        hm_ref[...] = _rms_rows(x1, gpre_ref[...]).astype(hm_ref.dtype)

    i = pl.program_id(0)
    pl.when(i < n_lo)(lambda: finish(xp_ref))
    pl.when(i >= n_lo)(lambda: finish(xs_ref))


def _outproj(o_rwkv, o_conv, xp, xs, p, tm=512):
    n = o_rwkv.shape[0]
    n_lo = xp.shape[0] // tm
    half = pl.BlockSpec((tm, D_RWKV), lambda i: (i, 0))
    row = pl.BlockSpec((tm, D_MODEL), lambda i: (i, 0))
    vec = pl.BlockSpec((1, D_MODEL), lambda i: (0, 0))
    return pl.pallas_call(
        functools.partial(_outproj_kernel, n_lo),
        grid=(n // tm,),
        in_specs=[half, half, _lo_spec(tm, D_MODEL, n_lo), _hi_spec(tm, D_MODEL, n_lo),
                  pl.BlockSpec((D_MODEL, D_MODEL), lambda i: (0, 0)), vec, vec],
        out_specs=[row, row],
        out_shape=[jax.ShapeDtypeStruct((n, D_MODEL), F32), jax.ShapeDtypeStruct((n, D_MODEL), BF16)],
        compiler_params=_cparams(("parallel",)),
        name="outproj",
    )(o_rwkv, o_conv, xp, xs, p["w_out"], p["g_post_mix"], p["g_pre_mlp"])


def _mlp_kernel(n_lo, hm_ref, x1_ref, wup_ref, wdn_ref, g_ref, op_ref, os_ref, acc_ref):
    i = pl.program_id(0)
    j = pl.program_id(1)
    @pl.when(j == 0)
    def _():
        acc_ref[...] = jnp.zeros_like(acc_ref)

    h = _dot(hm_ref[...], wup_ref[...])
    h = jnp.square(jnp.maximum(h, 0.0)).astype(BF16)
    acc_ref[...] += _dot(h, wdn_ref[...])

    def finish(o_ref):
        o_ref[...] = x1_ref[...] + _rms_rows(acc_ref[...], g_ref[...])

    is_last = j == pl.num_programs(1) - 1
    pl.when(is_last & (i < n_lo))(lambda: finish(op_ref))
    pl.when(is_last & (i >= n_lo))(lambda: finish(os_ref))


def _mlp(hm, x1, n_prompt, p, tm=512, tf=1024):
    n = hm.shape[0]
    n_lo = n_prompt // tm
    row = pl.BlockSpec((tm, D_MODEL), lambda i, j: (i, 0))
    return pl.pallas_call(
        functools.partial(_mlp_kernel, n_lo),
        grid=(n // tm, D_FF // tf),
        in_specs=[
            row, row,
            pl.BlockSpec((D_MODEL, tf), lambda i, j: (0, j)),
            pl.BlockSpec((tf, D_MODEL), lambda i, j: (j, 0)),
            pl.BlockSpec((1, D_MODEL), lambda i, j: (0, 0)),
        ],
        out_specs=[_lo_spec(tm, D_MODEL, n_lo), _hi_spec(tm, D_MODEL, n_lo)],
        out_shape=[jax.ShapeDtypeStruct((n_prompt, D_MODEL), F32),
                   jax.ShapeDtypeStruct((n - n_prompt, D_MODEL), F32)],
        scratch_shapes=[pltpu.VMEM((tm, D_MODEL), F32)],
        compiler_params=_cparams(("arbitrary", "arbitrary")),
        name="mlp",
    )(hm, x1, p["w_up"], p["w_down"], p["g_post_mlp"])


def _pad_rows(w, rows, offset):
    out = jnp.zeros((rows, w.shape[1]), w.dtype)
    return out.at[offset:offset + w.shape[0]].set(w)


def _layer_params(l, g_pre_mix, w_in, mu_prev, mu_next, w0_f, w2_f, w0_b, w2_b, a0_f, a2_f, a0_b, a2_b,
                  g2, k_k, k_a, r_k, gn_w, gn_b, dw_w, dw_b, cln_w, cln_b, w_out, g_post_mix,
                  g_pre_mlp, w_up, w_down, g_post_mlp):
    rwkv_cols = 3 * D_RWKV + LORA_COLS
    row = lambda a: a[l].reshape(1, -1).astype(F32)
    pad_cols = lambda a, width: jnp.pad(a, ((0, 0), (0, width - a.shape[1])))
    w = w_in[l].astype(BF16)
    heads = jnp.arange(D_RWKV) // HEAD_DIM
    return dict(
        g_pre_mix=row(g_pre_mix), w_zr=pad_cols(w[:, :rwkv_cols], ZR_COLS), w_zc=w[:, rwkv_cols:],
        mu_prev=pad_cols(row(mu_prev), ZR_COLS), mu_next=pad_cols(row(mu_next), ZR_COLS),
        w2=jnp.concatenate([_pad_rows(w2_f[l], 128, 0), _pad_rows(w2_b[l], 128, DECAY_LORA)], axis=1).astype(BF16),
        a2=jnp.concatenate([_pad_rows(a2_f[l], 128, 0), _pad_rows(a2_b[l], 128, AAA_LORA)], axis=1).astype(BF16),
        g2=_pad_rows(g2[l], 256, 0).astype(BF16),
        w0=jnp.concatenate([row(w0_f), row(w0_b)], axis=1),
        a0=jnp.concatenate([row(a0_f), row(a0_b)], axis=1),
        k_k=row(k_k), k_a=row(k_a), r_k=row(r_k), gn_w=row(gn_w), gn_b=row(gn_b),
        e=(heads[:, None] == heads[None, :]).astype(BF16),
        dw_w=dw_w[l].astype(F32), dw_b=row(dw_b), cln_w=row(cln_w), cln_b=row(cln_b),
        w_out=w_out[l].astype(BF16), g_post_mix=row(g_post_mix), g_pre_mlp=row(g_pre_mlp),
        w_up=w_up[l].astype(BF16), w_down=w_down[l].astype(BF16), g_post_mlp=row(g_post_mlp),
    )


def _layer(xp, xs, seq_len, p):
    n_seq = (xp.shape[0] + xs.shape[0]) // seq_len
    zr = _inproj(xp, xs, p["g_pre_mix"], p["w_zr"], "inproj_rwkv")
    zc = _inproj(xp, xs, p["g_pre_mix"], p["w_zc"], "inproj_conv")
    prep = _prep(zr, seq_len, p)
    g, bonus = prep[9], prep[10]
    yf, yb = _wkv(prep[:9], n_seq, seq_len)
    o_rwkv = _post(yf, yb, g, bonus, p)
    o_conv = _conv(zc, seq_len, p)
    x1, hm = _outproj(o_rwkv, o_conv, xp, xs, p)
    return _mlp(hm, x1, xp.shape[0], p)


def _forward(x_prompt, x_sample, weights):
    seq_len = x_prompt.shape[1]
    assert x_sample.shape[1] == seq_len
    xp = x_prompt.reshape(-1, D_MODEL)
    xs = x_sample.reshape(-1, D_MODEL)
    depth = weights[0].shape[0]
    for l in range(depth):
        xp, xs = _layer(xp, xs, seq_len, _layer_params(l, *weights))
    return xp.reshape(x_prompt.shape), xs.reshape(x_sample.shape)


def kernel(x_prompt, x_sample, g_pre_mix, w_in, mu_prev, mu_next, w0_f, w2_f, w0_b, w2_b, a0_f, a2_f, a0_b, a2_b, g2, k_k, k_a, r_k, gn_w, gn_b, dw_w, dw_b, cln_w, cln_b, w_out, g_post_mix, g_pre_mlp, w_up, w_down, g_post_mlp):
    weights = (g_pre_mix, w_in, mu_prev, mu_next, w0_f, w2_f, w0_b, w2_b, a0_f, a2_f, a0_b, a2_b,
               g2, k_k, k_a, r_k, gn_w, gn_b, dw_w, dw_b, cln_w, cln_b, w_out, g_post_mix,
               g_pre_mlp, w_up, w_down, g_post_mlp)
    return _forward(x_prompt, x_sample, weights)
```

```python
import functools
import math

import jax
import jax.numpy as jnp
from jax import lax
from jax.experimental import pallas as pl
from jax.experimental.pallas import tpu as pltpu

F32 = jnp.float32
BF16 = jnp.bfloat16

D_MODEL = 2048
D_RWKV = 1024
D_CONV = 1024
HEAD_DIM = 64
N_HEADS = 16
DECAY_LORA = 64
AAA_LORA = 64
GATE_LORA = 160
CONV_WIDTH = 31
CONV_HALF = CONV_WIDTH // 2
D_FF = 4 * D_MODEL
RMS_EPS = 1e-6
LN_EPS = 1e-5
GN_EPS = 64e-5
L2_EPS = 1e-12

LORA_COLS = 2 * DECAY_LORA + 2 * AAA_LORA + GATE_LORA
LORA_PAD = 512
ZR_COLS = 3 * D_RWKV + LORA_PAD
ZC_COLS = 2 * D_CONV

CHUNK = 64
GROUP = 256
N_GROUPS = D_RWKV // GROUP
WAVE_LAG = 3
HALO = 16
SUBLANES = 8
LANES = 128

VMEM_LIMIT = 56 * 1024 * 1024


def _cparams(sem):
    return pltpu.CompilerParams(dimension_semantics=sem, vmem_limit_bytes=VMEM_LIMIT)


def _dot(a, b):
    return jnp.dot(a, b, preferred_element_type=F32)


def _dot_nt(a, b):
    return lax.dot_general(a, b, (((1,), (1,)), ((), ())), preferred_element_type=F32)


def _dot_tn(a, b):
    return lax.dot_general(a, b, (((0,), (0,)), ((), ())), preferred_element_type=F32)


def _split2(x):
    hi = x.astype(BF16)
    lo = (x - hi.astype(F32)).astype(BF16)
    return hi, lo


def _seg_sum(x, e_ref):
    return _dot(x.astype(BF16), e_ref[...])


def _rms_rows(x, g):
    ms = jnp.mean(x * x, axis=-1, keepdims=True)
    return x * lax.rsqrt(ms + RMS_EPS) * g


def _lo_spec(tm, width, n_lo):
    return pl.BlockSpec((tm, width), lambda i, *_: (jnp.minimum(i, n_lo - 1), 0))


def _hi_spec(tm, width, n_lo):
    return pl.BlockSpec((tm, width), lambda i, *_: (jnp.maximum(i - n_lo, 0), 0))


def _inproj_kernel(n_lo, tn, xp_ref, xs_ref, w_ref, z_ref):
    def run(x_ref):
        x = x_ref[...]
        rinv = lax.rsqrt(jnp.mean(x * x, axis=-1, keepdims=True) + RMS_EPS)
        xb = x.astype(BF16)
        for c in range(z_ref.shape[1] // tn):
            cs = slice(c * tn, (c + 1) * tn)
            z_ref[:, cs] = (_dot(xb, w_ref[:, cs]) * rinv).astype(z_ref.dtype)

    i = pl.program_id(0)
    pl.when(i < n_lo)(lambda: run(xp_ref))
    pl.when(i >= n_lo)(lambda: run(xs_ref))


def _inproj(xp, xs, w, name, tm=512, tn=512):
    n_lo = xp.shape[0] // tm
    n = xp.shape[0] + xs.shape[0]
    cols = w.shape[1]
    return pl.pallas_call(
        functools.partial(_inproj_kernel, n_lo, tn),
        grid=(n // tm,),
        in_specs=[
            _lo_spec(tm, D_MODEL, n_lo),
            _hi_spec(tm, D_MODEL, n_lo),
            pl.BlockSpec((D_MODEL, cols), lambda i: (0, 0), pipeline_mode=pl.Buffered(1)),
        ],
        out_specs=pl.BlockSpec((tm, cols), lambda i: (i, 0)),
        out_shape=jax.ShapeDtypeStruct((n, cols), BF16),
        compiler_params=_cparams(("parallel",)),
        name=name,
    )(xp, xs, w)


def _prep_kernel(seq_len, tb,
                 z_ref, zp_ref, zn_ref, mup_ref, mun_ref, w2_ref, a2_ref, g2_ref,
                 w0_ref, a0_ref, kk_w_ref, ka_ref, rk_ref, e_ref,
                 r_ref, v_ref, kk_ref, lwf_ref, kmf_ref, bf_ref, lwb_ref, kmb_ref, bb_ref,
                 g_ref, bonus_ref):
    i = pl.program_id(0)
    t0 = (i * tb) % seq_len
    first = t0 == 0
    last = (t0 + tb) == seq_len
    edge = lax.broadcasted_iota(jnp.int32, (SUBLANES, 1), 0)

    def shifted(c0, c1):
        z = z_ref[:, c0:c1].astype(F32)
        mp = mup_ref[:, c0:c1]
        mn = mun_ref[:, c0:c1]
        body = (1.0 - mp - mn) * z + mp * pltpu.roll(z, 1, axis=0) + mn * pltpu.roll(z, tb - 1, axis=0)
        prev_row = jnp.where(first, 0.0, zp_ref[HALO - 1:HALO, c0:c1].astype(F32))
        next_row = jnp.where(last, 0.0, zn_ref[0:1, c0:c1].astype(F32))
        top = body[0:SUBLANES] + jnp.where(edge == 0, mp * (prev_row - z[tb - 1:tb]), 0.0)
        bot = body[tb - SUBLANES:tb] + jnp.where(edge == SUBLANES - 1, mn * (next_row - z[0:1]), 0.0)
        return jnp.concatenate([top, body[SUBLANES:tb - SUBLANES], bot], axis=0)

    r = shifted(0, D_RWKV)
    k = shifted(D_RWKV, 2 * D_RWKV)
    v = shifted(2 * D_RWKV, 3 * D_RWKV)
    lora = shifted(3 * D_RWKV, ZR_COLS)

    xw = jnp.tanh(lora[:, 0:128]).astype(BF16)
    xa = lora[:, 128:256].astype(BF16)
    xg = jax.nn.sigmoid(lora[:, 256:512]).astype(BF16)
    dw = _dot(xw, w2_ref[...]) + w0_ref[...]
    da = _dot(xa, a2_ref[...]) + a0_ref[...]
    g_ref[...] = _dot(xg, g2_ref[...]).astype(g_ref.dtype)

    kraw = k * kk_w_ref[...]
    ss = _seg_sum(kraw * kraw, e_ref)
    kk = kraw * lax.rsqrt(jnp.maximum(ss, L2_EPS * L2_EPS))
    rk = _seg_sum(r * k * rk_ref[...], e_ref)
    bonus_ref[...] = (rk * v).astype(bonus_ref.dtype)
    r_ref[...] = r.astype(r_ref.dtype)
    v_ref[...] = v.astype(v_ref.dtype)
    kk_ref[...] = kk.astype(kk_ref.dtype)

    neg_c = -math.exp(-0.5)
    ka = ka_ref[...]
    for d, (lw_ref, km_ref, b_ref) in enumerate(((lwf_ref, kmf_ref, bf_ref), (lwb_ref, kmb_ref, bb_ref))):
        sl = slice(d * D_RWKV, (d + 1) * D_RWKV)
        lw_ref[...] = neg_c * jax.nn.sigmoid(dw[:, sl])
        a = jax.nn.sigmoid(da[:, sl])
        km_ref[...] = (k * (1.0 + (a - 1.0) * ka)).astype(km_ref.dtype)
        b_ref[...] = (kk * a).astype(b_ref.dtype)


_PREP_OUT_DTYPES = (BF16, BF16, BF16, F32, BF16, BF16, F32, BF16, BF16, BF16, BF16)


def _prep(zr, seq_len, p, tb=256):
    n = zr.shape[0]
    hb = tb // HALO
    nhb = n // HALO
    vec = lambda w: pl.BlockSpec((1, w), lambda i: (0, 0))
    full = lambda a: pl.BlockSpec(a.shape, lambda i: (0, 0))
    out_spec = pl.BlockSpec((tb, D_RWKV), lambda i: (i, 0))
    return pl.pallas_call(
        functools.partial(_prep_kernel, seq_len, tb),
        grid=(n // tb,),
        in_specs=[
            pl.BlockSpec((tb, ZR_COLS), lambda i: (i, 0)),
            pl.BlockSpec((HALO, ZR_COLS), lambda i: (jnp.maximum(i * hb - 1, 0), 0)),
            pl.BlockSpec((HALO, ZR_COLS), lambda i: (jnp.minimum((i + 1) * hb, nhb - 1), 0)),
            vec(ZR_COLS), vec(ZR_COLS),
            full(p["w2"]), full(p["a2"]), full(p["g2"]),
            vec(2 * D_RWKV), vec(2 * D_RWKV), vec(D_RWKV), vec(D_RWKV), vec(D_RWKV),
            full(p["e"]),
        ],
        out_specs=[out_spec] * 11,
        out_shape=[jax.ShapeDtypeStruct((n, D_RWKV), dt) for dt in _PREP_OUT_DTYPES],
        compiler_params=_cparams(("parallel",)),
        name="rwkv_prep",
    )(zr, zr, zr, p["mu_prev"], p["mu_next"], p["w2"], p["a2"], p["g2"],
      p["w0"], p["a0"], p["k_k"], p["k_a"], p["r_k"], p["e"])


def _wkv_masks(reverse):
    t = lax.broadcasted_iota(jnp.int32, (CHUNK, GROUP), 0)
    lane = lax.broadcasted_iota(jnp.int32, (CHUNK, GROUP), 1)
    s = lane % CHUNK
    rr = lax.broadcasted_iota(jnp.int32, (LANES, LANES), 0)
    cc = lax.broadcasted_iota(jnp.int32, (LANES, LANES), 1)
    ti = lax.broadcasted_iota(jnp.int32, (CHUNK, CHUNK), 0)
    si = lax.broadcasted_iota(jnp.int32, (CHUNK, CHUNK), 1)
    return dict(
        strict=(s > t) if reverse else (s < t),
        incl=(s >= t) if reverse else (s <= t),
        eye=(s == t).astype(F32),
        bd=(rr // HEAD_DIM) == (cc // HEAD_DIM),
        first_head=lax.broadcasted_iota(jnp.int32, (CHUNK, LANES), 1) < HEAD_DIM,
        tri=((si >= ti) if reverse else (si <= ti)).astype(BF16),
    )


def _bd(x, m):
    zero = jnp.zeros((), x.dtype)
    quad = lambda h: jnp.where(m["bd"], jnp.concatenate([x[:, h * LANES:(h + 1) * LANES]] * 2, axis=0), zero)
    zq = jnp.zeros((LANES, LANES), x.dtype)
    return jnp.concatenate([jnp.concatenate([quad(0), zq], axis=0),
                            jnp.concatenate([zq, quad(1)], axis=0)], axis=1)


def _diag_blocks(full, m):
    halves = []
    for half in range(GROUP // LANES):
        lanes = slice(half * LANES, (half + 1) * LANES)
        r0 = half * LANES
        halves.append(jnp.where(m["first_head"], full[r0:r0 + CHUNK, lanes], full[r0 + CHUNK:r0 + LANES, lanes]))
    return jnp.concatenate(halves, axis=1)


def _wkv_stage0(refs, off, reverse, m):
    r_ref, v_ref, kk_ref, lw_ref, km_ref, b_ref = refs
    sl = pl.ds(off, CHUNK)
    lw = lw_ref[sl, :]
    hi, lo = _split2(lw)
    tri = m["tri"]
    cum = _dot(tri, hi) + _dot(tri, lo)
    tot = cum[0:1, :] if reverse else cum[CHUNK - 1:CHUNK, :]
    e_neg = jnp.exp(-cum)
    e_rest = jnp.exp(tot - cum)
    b = b_ref[sl, :].astype(F32)
    km = km_ref[sl, :].astype(F32)
    return dict(
        a_t=-kk_ref[sl, :].astype(F32) * jnp.exp(cum - lw),
        r_t=r_ref[sl, :].astype(F32) * jnp.exp(cum),
        b_h=b * e_neg, k_h=km * e_neg, b_r=b * e_rest, k_r=km * e_rest,
        v=v_ref[sl, :], e_tot=jnp.exp(tot),
    )


def _wkv_wave(fwd_refs, bwd_refs, off_f, off_b, mf, mb, yf_ref, yb_ref, sf_ref, sb_ref):
    pre_f = _wkv_stage0(fwd_refs, off_f, False, mf)
    pre_b = _wkv_stage0(bwd_refs, off_b, True, mb)
    yield
    insts = []
    for g in range(N_GROUPS):
        insts.append(dict(pre=pre_f, g=g, m=mf, s_ref=sf_ref, y_ref=yf_ref, sl=pl.ds(off_f, CHUNK)))
        insts.append(dict(pre=pre_b, g=g, m=mb, s_ref=sb_ref, y_ref=yb_ref, sl=pl.ds(off_b, CHUNK)))
    rng = range(len(insts))
    ms = [it["m"] for it in insts]
    ls = [slice(it["g"] * GROUP, (it["g"] + 1) * GROUP) for it in insts]
    pre = [it["pre"] for it in insts]
    rt = [pre[i]["r_t"][:, ls[i]] for i in rng]
    at_b = [pre[i]["a_t"][:, ls[i]].astype(BF16) for i in rng]
    v_b = [pre[i]["v"][:, ls[i]].astype(BF16) for i in rng]
    lhs = [jnp.concatenate([at_b[i], rt[i].astype(BF16)], axis=0) for i in rng]
    gb = [_dot_nt(lhs[i], _bd(pre[i]["b_h"][:, ls[i]].astype(BF16), ms[i])) for i in rng]
    yield
    gk = [_dot_nt(lhs[i], _bd(pre[i]["k_h"][:, ls[i]].astype(BF16), ms[i])) for i in rng]
    a_ab = [jnp.where(ms[i]["strict"], gb[i][:CHUNK], 0.0) for i in rng]
    a_rb = [jnp.where(ms[i]["incl"], gb[i][CHUNK:], 0.0).astype(BF16) for i in rng]
    yield
    t_inv = [ms[i]["eye"] + a_ab[i] for i in rng]
    xb = [a.astype(BF16) for a in a_ab]
    x = [_dot(xb[i], _bd(xb[i], ms[i])) for i in rng]
    a_k = [jnp.concatenate([jnp.where(ms[i]["strict"], gk[i][:CHUNK], 0.0),
                            jnp.where(ms[i]["incl"], gk[i][CHUNK:], 0.0)], axis=0).astype(BF16) for i in rng]
    av = [_dot(a_k[i], _bd(v_b[i], ms[i])) for i in rng]
    yield
    for _ in range(int(math.log2(CHUNK)) - 2):
        xb = [xi.astype(BF16) for xi in x]
        res = [_dot(jnp.concatenate([xb[i], t_inv[i].astype(BF16)], axis=0), _bd(xb[i], ms[i])) for i in rng]
        x = [r[:CHUNK] for r in res]
        t_inv = [t_inv[i] + res[i][CHUNK:] for i in rng]
        yield
    res = [_dot(t_inv[i].astype(BF16), _bd(x[i].astype(BF16), ms[i])) for i in rng]
    t_b = [(t_inv[i] + res[i]).astype(BF16) for i in rng]
    yield
    u0 = [_dot(t_b[i], _bd(av[i][:CHUNK].astype(BF16), ms[i])) for i in rng]
    w = [_dot(t_b[i], _bd(at_b[i], ms[i])) for i in rng]
    yield
    q = [rt[i] + _dot(a_rb[i], _bd(w[i].astype(BF16), ms[i])) for i in rng]
    y1 = [av[i][CHUNK:] + _dot(a_rb[i], _bd(u0[i].astype(BF16), ms[i])) for i in rng]
    yield
    bk_b = [jnp.concatenate([pre[i]["b_r"][:, ls[i]], pre[i]["k_r"][:, ls[i]]], axis=0).astype(BF16) for i in rng]
    wqd = [jnp.concatenate([w[i], q[i], ms[i]["eye"] * pre[i]["e_tot"][:, ls[i]]], axis=0).astype(BF16)
           for i in rng]
    res = [_dot(wqd[i], _bd(insts[i]["s_ref"][insts[i]["g"]].astype(BF16), ms[i])) for i in rng]
    for i in rng:
        it = insts[i]
        it["y_ref"][it["sl"], ls[i]] = (res[i][CHUNK:2 * CHUNK] + y1[i]).astype(it["y_ref"].dtype)
    yield
    upd = [_dot_tn(bk_b[i], jnp.concatenate([(res[i][:CHUNK] + u0[i]).astype(BF16), v_b[i]], axis=0))
           for i in rng]
    for i in rng:
        insts[i]["s_ref"][insts[i]["g"]] = res[i][2 * CHUNK:] + _diag_blocks(upd[i], ms[i])


def _interleave(gens, lag):
    pending = list(gens)
    active = []
    tick = 0
    while pending or active:
        if pending and tick % lag == 0:
            active.append(pending.pop(0))
        for g in list(active):
            if next(g, StopIteration) is StopIteration:
                active.remove(g)
        tick += 1


def _wkv_kernel(n_chunks, *refs):
    fwd_refs = refs[0:6]
    bwd_refs = refs[6:12]
    yf_ref, yb_ref, sf_ref, sb_ref = refs[12:16]

    @pl.when(pl.program_id(1) == 0)
    def _():
        sf_ref[...] = jnp.zeros_like(sf_ref)
        sb_ref[...] = jnp.zeros_like(sb_ref)

    mf = _wkv_masks(False)
    mb = _wkv_masks(True)
    waves = [_wkv_wave(fwd_refs, bwd_refs, c * CHUNK, (n_chunks - 1 - c) * CHUNK, mf, mb,
                       yf_ref, yb_ref, sf_ref, sb_ref) for c in range(n_chunks)]
    _interleave(waves, WAVE_LAG)


def _wkv(pp, n_seq, seq_len, tb=512):
    (r, v, kk, lwf, kmf, bf, lwb, kmb, bb) = pp
    n = r.shape[0]
    nb = seq_len // tb
    fspec = pl.BlockSpec((tb, D_RWKV), lambda s, i: (s * nb + i, 0))
    bspec = pl.BlockSpec((tb, D_RWKV), lambda s, i: (s * nb + nb - 1 - i, 0))
    return pl.pallas_call(
        functools.partial(_wkv_kernel, tb // CHUNK),
        grid=(n_seq, nb),
        in_specs=[fspec] * 6 + [bspec] * 6,
        out_specs=[fspec, bspec],
        out_shape=[jax.ShapeDtypeStruct((n, D_RWKV), BF16)] * 2,
        scratch_shapes=[pltpu.VMEM((N_GROUPS, CHUNK, GROUP), F32)] * 2,
        compiler_params=_cparams(("parallel", "arbitrary")),
        name="wkv",
    )(r, v, kk, lwf, kmf, bf, r, v, kk, lwb, kmb, bb)


def _post_kernel(yf_ref, yb_ref, g_ref, bonus_ref, gnw_ref, gnb_ref, e_ref, o_ref):
    y = yf_ref[...].astype(F32) + yb_ref[...].astype(F32)
    mu = _seg_sum(y, e_ref) * (1.0 / HEAD_DIM)
    d = y - mu
    var = _seg_sum(d * d, e_ref) * (1.0 / HEAD_DIM)
    yn = d * lax.rsqrt(var + GN_EPS) * gnw_ref[...] + gnb_ref[...]
    o_ref[...] = ((yn + bonus_ref[...].astype(F32)) * g_ref[...].astype(F32)).astype(o_ref.dtype)


def _post(yf, yb, g, bonus, p, tb=512):
    n = yf.shape[0]
    spec = pl.BlockSpec((tb, D_RWKV), lambda i: (i, 0))
    vec = pl.BlockSpec((1, D_RWKV), lambda i: (0, 0))
    return pl.pallas_call(
        _post_kernel,
        grid=(n // tb,),
        in_specs=[spec] * 4 + [vec, vec, pl.BlockSpec(p["e"].shape, lambda i: (0, 0))],
        out_specs=spec,
        out_shape=jax.ShapeDtypeStruct((n, D_RWKV), BF16),
        compiler_params=_cparams(("parallel",)),
        name="rwkv_post",
    )(yf, yb, g, bonus, p["gn_w"], p["gn_b"], p["e"])


CONV_ROWS = 64


def _conv_kernel(seq_len, tb, zc_ref, zp_ref, zn_ref, dww_ref, dwb_ref, lnw_ref, lnb_ref,
                 o_ref, u_scr, sh_scr, c_scr):
    i = pl.program_id(0)
    t0 = (i * tb) % seq_len
    first = t0 == 0
    last = (t0 + tb) == seq_len

    def glu(ref):
        return ref[:, 0:D_CONV].astype(F32) * jax.nn.sigmoid(ref[:, D_CONV:2 * D_CONV].astype(F32))

    u_scr[HALO:HALO + tb, :] = glu(zc_ref)
    u_scr[0:HALO, :] = jnp.where(first, 0.0, glu(zp_ref))
    u_scr[HALO + tb:2 * HALO + tb, :] = jnp.where(last, 0.0, glu(zn_ref))

    sh_rows = tb + 2 * HALO - SUBLANES
    for s in range(1, SUBLANES):
        sh_scr[s - 1] = u_scr[s:s + sh_rows, :]

    base = HALO - CONV_HALF
    for cb in range(D_CONV // LANES):
        cs = slice(cb * LANES, (cb + 1) * LANES)
        wcol = dww_ref[:, cs]
        bias = dwb_ref[:, cs]
        for rc in range(tb // CONV_ROWS):
            r0 = rc * CONV_ROWS
            acc = jnp.zeros((CONV_ROWS, LANES), F32) + bias
            for j in range(CONV_WIDTH):
                s = (base + j) % SUBLANES
                q = r0 + base + j - s
                if s == 0:
                    u = u_scr[q:q + CONV_ROWS, cs]
                else:
                    u = sh_scr[s - 1, q:q + CONV_ROWS, cs]
                acc = acc + u * wcol[j:j + 1, :]
            c_scr[r0:r0 + CONV_ROWS, cs] = acc

    c = c_scr[...]
    cm = jnp.mean(c, axis=-1, keepdims=True)
    d = c - cm
    cv = jnp.mean(d * d, axis=-1, keepdims=True)
    y = d * lax.rsqrt(cv + LN_EPS) * lnw_ref[...] + lnb_ref[...]
    o_ref[...] = (y * jax.nn.sigmoid(y)).astype(o_ref.dtype)


def _conv(zc, seq_len, p, tb=256):
    n = zc.shape[0]
    hb = tb // HALO
    nhb = n // HALO
    vec = pl.BlockSpec((1, D_CONV), lambda i: (0, 0))
    return pl.pallas_call(
        functools.partial(_conv_kernel, seq_len, tb),
        grid=(n // tb,),
        in_specs=[
            pl.BlockSpec((tb, ZC_COLS), lambda i: (i, 0)),
            pl.BlockSpec((HALO, ZC_COLS), lambda i: (jnp.maximum(i * hb - 1, 0), 0)),
            pl.BlockSpec((HALO, ZC_COLS), lambda i: (jnp.minimum((i + 1) * hb, nhb - 1), 0)),
            pl.BlockSpec((CONV_WIDTH, D_CONV), lambda i: (0, 0)),
            vec, vec, vec,
        ],
        out_specs=pl.BlockSpec((tb, D_CONV), lambda i: (i, 0)),
        out_shape=jax.ShapeDtypeStruct((n, D_CONV), BF16),
        scratch_shapes=[
            pltpu.VMEM((tb + 2 * HALO, D_CONV), F32),
            pltpu.VMEM((SUBLANES - 1, tb + 2 * HALO - SUBLANES, D_CONV), F32),
            pltpu.VMEM((tb, D_CONV), F32),
        ],
        compiler_params=_cparams(("parallel",)),
        name="conv_module",
    )(zc, zc, zc, p["dw_w"], p["dw_b"], p["cln_w"], p["cln_b"])


OUTPROJ_ROWS = 256


def _outproj_kernel(n_lo, orw_ref, ocv_ref, xp_ref, xs_ref, w_ref, gpost_ref, gpre_ref, x1_ref, hm_ref):
    from_prompt = pl.program_id(0) < n_lo
    tm = x1_ref.shape[0]
    for r0 in range(0, tm, OUTPROJ_ROWS):
        rs = slice(r0, r0 + OUTPROJ_ROWS)
        mix = _dot(orw_ref[rs, :], w_ref[0:D_RWKV, :]) + _dot(ocv_ref[rs, :], w_ref[D_RWKV:D_MODEL, :])
        x = jnp.where(from_prompt, xp_ref[rs, :], xs_ref[rs, :])
        x1 = x + _rms_rows(mix, gpost_ref[...])
        x1_ref[rs, :] = x1
        hm_ref[rs, :] = _rms_rows(x1, gpre_ref[...]).astype(hm_ref.dtype)


def _outproj(o_rwkv, o_conv, xp, xs, p, tm=512):
    n = o_rwkv.shape[0]
    n_lo = xp.shape[0] // tm
    half = pl.BlockSpec((tm, D_RWKV), lambda i: (i, 0))
    row = pl.BlockSpec((tm, D_MODEL), lambda i: (i, 0))
    vec = pl.BlockSpec((1, D_MODEL), lambda i: (0, 0))
    return pl.pallas_call(
        functools.partial(_outproj_kernel, n_lo),
        grid=(n // tm,),
        in_specs=[half, half, _lo_spec(tm, D_MODEL, n_lo), _hi_spec(tm, D_MODEL, n_lo),
                  pl.BlockSpec((D_MODEL, D_MODEL), lambda i: (0, 0)), vec, vec],
        out_specs=[row, row],
        out_shape=[jax.ShapeDtypeStruct((n, D_MODEL), F32), jax.ShapeDtypeStruct((n, D_MODEL), BF16)],
        compiler_params=_cparams(("parallel",)),
        name="outproj",
    )(o_rwkv, o_conv, xp, xs, p["w_out"], p["g_post_mix"], p["g_pre_mlp"])


def _mlp_kernel(n_lo, hm_ref, x1_ref, wup_ref, wdn_ref, g_ref, op_ref, os_ref, acc_ref):
    i = pl.program_id(0)
    j = pl.program_id(1)
    @pl.when(j == 0)
    def _():
        acc_ref[...] = jnp.zeros_like(acc_ref)

    h = _dot(hm_ref[...], wup_ref[...])
    h = jnp.square(jnp.maximum(h, 0.0)).astype(BF16)
    acc_ref[...] += _dot(h, wdn_ref[...])

    def finish(o_ref):
        o_ref[...] = x1_ref[...] + _rms_rows(acc_ref[...], g_ref[...])

    is_last = j == pl.num_programs(1) - 1
    pl.when(is_last & (i < n_lo))(lambda: finish(op_ref))
    pl.when(is_last & (i >= n_lo))(lambda: finish(os_ref))


def _mlp(hm, x1, n_prompt, p, tm=512, tf=1024):
    n = hm.shape[0]
    n_lo = n_prompt // tm
    row = pl.BlockSpec((tm, D_MODEL), lambda i, j: (i, 0))
    return pl.pallas_call(
        functools.partial(_mlp_kernel, n_lo),
        grid=(n // tm, D_FF // tf),
        in_specs=[
            row, row,
            pl.BlockSpec((D_MODEL, tf), lambda i, j: (0, j)),
            pl.BlockSpec((tf, D_MODEL), lambda i, j: (j, 0)),
            pl.BlockSpec((1, D_MODEL), lambda i, j: (0, 0)),
        ],
        out_specs=[_lo_spec(tm, D_MODEL, n_lo), _hi_spec(tm, D_MODEL, n_lo)],
        out_shape=[jax.ShapeDtypeStruct((n_prompt, D_MODEL), F32),
                   jax.ShapeDtypeStruct((n - n_prompt, D_MODEL), F32)],
        scratch_shapes=[pltpu.VMEM((tm, D_MODEL), F32)],
        compiler_params=_cparams(("arbitrary", "arbitrary")),
        name="mlp",
    )(hm, x1, p["w_up"], p["w_down"], p["g_post_mlp"])


def _pad_rows(w, rows, offset):
    out = jnp.zeros((rows, w.shape[1]), w.dtype)
    return out.at[offset:offset + w.shape[0]].set(w)


def _layer_params(l, g_pre_mix, w_in, mu_prev, mu_next, w0_f, w2_f, w0_b, w2_b, a0_f, a2_f, a0_b, a2_b,
                  g2, k_k, k_a, r_k, gn_w, gn_b, dw_w, dw_b, cln_w, cln_b, w_out, g_post_mix,
                  g_pre_mlp, w_up, w_down, g_post_mlp):
    rwkv_cols = 3 * D_RWKV + LORA_COLS
    row = lambda a: a[l].reshape(1, -1).astype(F32)
    pad_cols = lambda a, width: jnp.pad(a, ((0, 0), (0, width - a.shape[1])))
    w = (g_pre_mix[l].astype(F32)[:, None] * w_in[l]).astype(BF16)
    heads = jnp.arange(D_RWKV) // HEAD_DIM
    return dict(
        w_zr=pad_cols(w[:, :rwkv_cols], ZR_COLS), w_zc=w[:, rwkv_cols:],
        mu_prev=pad_cols(row(mu_prev), ZR_COLS), mu_next=pad_cols(row(mu_next), ZR_COLS),
        w2=jnp.concatenate([_pad_rows(w2_f[l], 128, 0), _pad_rows(w2_b[l], 128, DECAY_LORA)], axis=1).astype(BF16),
        a2=jnp.concatenate([_pad_rows(a2_f[l], 128, 0), _pad_rows(a2_b[l], 128, AAA_LORA)], axis=1).astype(BF16),
        g2=_pad_rows(g2[l], 256, 0).astype(BF16),
        w0=jnp.concatenate([row(w0_f), row(w0_b)], axis=1),
        a0=jnp.concatenate([row(a0_f), row(a0_b)], axis=1),
        k_k=row(k_k), k_a=row(k_a), r_k=row(r_k), gn_w=row(gn_w), gn_b=row(gn_b),
        e=(heads[:, None] == heads[None, :]).astype(BF16),
        dw_w=dw_w[l].astype(F32), dw_b=row(dw_b), cln_w=row(cln_w), cln_b=row(cln_b),
        w_out=w_out[l].astype(BF16), g_post_mix=row(g_post_mix), g_pre_mlp=row(g_pre_mlp),
        w_up=w_up[l].astype(BF16), w_down=w_down[l].astype(BF16), g_post_mlp=row(g_post_mlp),
    )


def _layer(xp, xs, seq_len, p):
    n_seq = (xp.shape[0] + xs.shape[0]) // seq_len
    zr = _inproj(xp, xs, p["w_zr"], "inproj_rwkv")
    zc = _inproj(xp, xs, p["w_zc"], "inproj_conv")
    prep = _prep(zr, seq_len, p)
    g, bonus = prep[9], prep[10]
    yf, yb = _wkv(prep[:9], n_seq, seq_len)
    o_rwkv = _post(yf, yb, g, bonus, p)
    o_conv = _conv(zc, seq_len, p)
    x1, hm = _outproj(o_rwkv, o_conv, xp, xs, p)
    return _mlp(hm, x1, xp.shape[0], p)


def _forward(x_prompt, x_sample, weights):
    seq_len = x_prompt.shape[1]
    assert x_sample.shape[1] == seq_len
    xp = x_prompt.reshape(-1, D_MODEL)
    xs = x_sample.reshape(-1, D_MODEL)
    depth = weights[0].shape[0]
    for l in range(depth):
        xp, xs = _layer(xp, xs, seq_len, _layer_params(l, *weights))
    return xp.reshape(x_prompt.shape), xs.reshape(x_sample.shape)


def kernel(x_prompt, x_sample, g_pre_mix, w_in, mu_prev, mu_next, w0_f, w2_f, w0_b, w2_b, a0_f, a2_f, a0_b, a2_b, g2, k_k, k_a, r_k, gn_w, gn_b, dw_w, dw_b, cln_w, cln_b, w_out, g_post_mix, g_pre_mlp, w_up, w_down, g_post_mlp):
    weights = (g_pre_mix, w_in, mu_prev, mu_next, w0_f, w2_f, w0_b, w2_b, a0_f, a2_f, a0_b, a2_b,
               g2, k_k, k_a, r_k, gn_w, gn_b, dw_w, dw_b, cln_w, cln_b, w_out, g_post_mix,
               g_pre_mlp, w_up, w_down, g_post_mlp)
    return _forward(x_prompt, x_sample, weights)
```

```python
import functools
import math

import jax
import jax.numpy as jnp
from jax import lax
from jax.experimental import pallas as pl
from jax.experimental.pallas import tpu as pltpu

F32 = jnp.float32
BF16 = jnp.bfloat16

D_MODEL = 2048
D_RWKV = 1024
D_CONV = 1024
HEAD_DIM = 64
N_HEADS = 16
DECAY_LORA = 64
AAA_LORA = 64
GATE_LORA = 160
CONV_WIDTH = 31
CONV_HALF = CONV_WIDTH // 2
D_FF = 4 * D_MODEL
RMS_EPS = 1e-6
LN_EPS = 1e-5
GN_EPS = 64e-5
L2_EPS = 1e-12

LORA_COLS = 2 * DECAY_LORA + 2 * AAA_LORA + GATE_LORA
LORA_PAD = 512
ZR_COLS = 3 * D_RWKV + LORA_PAD

CHUNK = 64
GROUP = 256
N_GROUPS = D_RWKV // GROUP
WAVE_LAG = 3
HALO = 16
SUBLANES = 8
LANES = 128

VMEM_LIMIT = 56 * 1024 * 1024


def _cparams(sem):
    return pltpu.CompilerParams(dimension_semantics=sem, vmem_limit_bytes=VMEM_LIMIT)


def _dot(a, b):
    return jnp.dot(a, b, preferred_element_type=F32)


def _dot_nt(a, b):
    return lax.dot_general(a, b, (((1,), (1,)), ((), ())), preferred_element_type=F32)


def _dot_tn(a, b):
    return lax.dot_general(a, b, (((0,), (0,)), ((), ())), preferred_element_type=F32)


def _split2(x):
    hi = x.astype(BF16)
    lo = (x - hi.astype(F32)).astype(BF16)
    return hi, lo


def _seg_sum(x, e_ref):
    return _dot(x.astype(BF16), e_ref[...])


def _rms_rows(x, g):
    ms = jnp.mean(x * x, axis=-1, keepdims=True)
    return x * lax.rsqrt(ms + RMS_EPS) * g


def _lo_spec(tm, width, n_lo):
    return pl.BlockSpec((tm, width), lambda i, *_: (jnp.minimum(i, n_lo - 1), 0))


def _hi_spec(tm, width, n_lo):
    return pl.BlockSpec((tm, width), lambda i, *_: (jnp.maximum(i - n_lo, 0), 0))


def _inproj_kernel(n_lo, tn, glu, xp_ref, xs_ref, w_ref, z_ref):
    def run(x_ref):
        x = x_ref[...]
        rinv = lax.rsqrt(jnp.mean(x * x, axis=-1, keepdims=True) + RMS_EPS)
        xb = x.astype(BF16)
        out_cols = z_ref.shape[1]
        for c in range(out_cols // tn):
            cs = slice(c * tn, (c + 1) * tn)
            z = _dot(xb, w_ref[:, cs]) * rinv
            if glu:
                gate = _dot(xb, w_ref[:, out_cols + c * tn:out_cols + (c + 1) * tn]) * rinv
                z = z * jax.nn.sigmoid(gate)
            z_ref[:, cs] = z.astype(z_ref.dtype)

    i = pl.program_id(0)
    pl.when(i < n_lo)(lambda: run(xp_ref))
    pl.when(i >= n_lo)(lambda: run(xs_ref))


def _inproj(xp, xs, w, name, glu=False, tm=512, tn=512):
    n_lo = xp.shape[0] // tm
    n = xp.shape[0] + xs.shape[0]
    cols = w.shape[1]
    out_cols = cols // 2 if glu else cols
    return pl.pallas_call(
        functools.partial(_inproj_kernel, n_lo, tn, glu),
        grid=(n // tm,),
        in_specs=[
            _lo_spec(tm, D_MODEL, n_lo),
            _hi_spec(tm, D_MODEL, n_lo),
            pl.BlockSpec((D_MODEL, cols), lambda i: (0, 0), pipeline_mode=pl.Buffered(1)),
        ],
        out_specs=pl.BlockSpec((tm, out_cols), lambda i: (i, 0)),
        out_shape=jax.ShapeDtypeStruct((n, out_cols), BF16),
        compiler_params=_cparams(("parallel",)),
        name=name,
    )(xp, xs, w)


def _prep_kernel(seq_len, tb,
                 z_ref, zp_ref, zn_ref, mup_ref, mun_ref, w2_ref, a2_ref, g2_ref,
                 w0_ref, a0_ref, kk_w_ref, ka_ref, rk_ref, e_ref,
                 r_ref, v_ref, kk_ref, lwf_ref, kmf_ref, bf_ref, lwb_ref, kmb_ref, bb_ref,
                 g_ref, bonus_ref):
    i = pl.program_id(0)
    t0 = (i * tb) % seq_len
    first = t0 == 0
    last = (t0 + tb) == seq_len
    edge = lax.broadcasted_iota(jnp.int32, (SUBLANES, 1), 0)

    def shifted(c0, c1):
        z = z_ref[:, c0:c1].astype(F32)
        mp = mup_ref[:, c0:c1]
        mn = mun_ref[:, c0:c1]
        body = (1.0 - mp - mn) * z + mp * pltpu.roll(z, 1, axis=0) + mn * pltpu.roll(z, tb - 1, axis=0)
        prev_row = jnp.where(first, 0.0, zp_ref[HALO - 1:HALO, c0:c1].astype(F32))
        next_row = jnp.where(last, 0.0, zn_ref[0:1, c0:c1].astype(F32))
        top = body[0:SUBLANES] + jnp.where(edge == 0, mp * (prev_row - z[tb - 1:tb]), 0.0)
        bot = body[tb - SUBLANES:tb] + jnp.where(edge == SUBLANES - 1, mn * (next_row - z[0:1]), 0.0)
        return jnp.concatenate([top, body[SUBLANES:tb - SUBLANES], bot], axis=0)

    r = shifted(0, D_RWKV)
    k = shifted(D_RWKV, 2 * D_RWKV)
    v = shifted(2 * D_RWKV, 3 * D_RWKV)
    lora = shifted(3 * D_RWKV, ZR_COLS)

    xw = jnp.tanh(lora[:, 0:128]).astype(BF16)
    xa = lora[:, 128:256].astype(BF16)
    xg = jax.nn.sigmoid(lora[:, 256:512]).astype(BF16)
    dw = _dot(xw, w2_ref[...]) + w0_ref[...]
    da = _dot(xa, a2_ref[...]) + a0_ref[...]
    g_ref[...] = _dot(xg, g2_ref[...]).astype(g_ref.dtype)

    kraw = k * kk_w_ref[...]
    ss = _seg_sum(kraw * kraw, e_ref)
    kk = kraw * lax.rsqrt(jnp.maximum(ss, L2_EPS * L2_EPS))
    rk = _seg_sum(r * k * rk_ref[...], e_ref)
    bonus_ref[...] = (rk * v).astype(bonus_ref.dtype)
    r_ref[...] = r.astype(r_ref.dtype)
    v_ref[...] = v.astype(v_ref.dtype)
    kk_ref[...] = kk.astype(kk_ref.dtype)

    neg_c = -math.exp(-0.5)
    ka = ka_ref[...]
    for d, (lw_ref, km_ref, b_ref) in enumerate(((lwf_ref, kmf_ref, bf_ref), (lwb_ref, kmb_ref, bb_ref))):
        sl = slice(d * D_RWKV, (d + 1) * D_RWKV)
        lw_ref[...] = neg_c * jax.nn.sigmoid(dw[:, sl])
        a = jax.nn.sigmoid(da[:, sl])
        km_ref[...] = (k * (1.0 + (a - 1.0) * ka)).astype(km_ref.dtype)
        b_ref[...] = (kk * a).astype(b_ref.dtype)


_PREP_OUT_DTYPES = (BF16, BF16, BF16, F32, BF16, BF16, F32, BF16, BF16, BF16, BF16)


def _prep(zr, seq_len, p, tb=256):
    n = zr.shape[0]
    hb = tb // HALO
    nhb = n // HALO
    vec = lambda w: pl.BlockSpec((1, w), lambda i: (0, 0))
    full = lambda a: pl.BlockSpec(a.shape, lambda i: (0, 0))
    out_spec = pl.BlockSpec((tb, D_RWKV), lambda i: (i, 0))
    return pl.pallas_call(
        functools.partial(_prep_kernel, seq_len, tb),
        grid=(n // tb,),
        in_specs=[
            pl.BlockSpec((tb, ZR_COLS), lambda i: (i, 0)),
            pl.BlockSpec((HALO, ZR_COLS), lambda i: (jnp.maximum(i * hb - 1, 0), 0)),
            pl.BlockSpec((HALO, ZR_COLS), lambda i: (jnp.minimum((i + 1) * hb, nhb - 1), 0)),
            vec(ZR_COLS), vec(ZR_COLS),
            full(p["w2"]), full(p["a2"]), full(p["g2"]),
            vec(2 * D_RWKV), vec(2 * D_RWKV), vec(D_RWKV), vec(D_RWKV), vec(D_RWKV),
            full(p["e"]),
        ],
        out_specs=[out_spec] * 11,
        out_shape=[jax.ShapeDtypeStruct((n, D_RWKV), dt) for dt in _PREP_OUT_DTYPES],
        compiler_params=_cparams(("parallel",)),
        name="rwkv_prep",
    )(zr, zr, zr, p["mu_prev"], p["mu_next"], p["w2"], p["a2"], p["g2"],
      p["w0"], p["a0"], p["k_k"], p["k_a"], p["r_k"], p["e"])


def _wkv_masks(reverse):
    t = lax.broadcasted_iota(jnp.int32, (CHUNK, GROUP), 0)
    lane = lax.broadcasted_iota(jnp.int32, (CHUNK, GROUP), 1)
    s = lane % CHUNK
    rr = lax.broadcasted_iota(jnp.int32, (LANES, LANES), 0)
    cc = lax.broadcasted_iota(jnp.int32, (LANES, LANES), 1)
    ti = lax.broadcasted_iota(jnp.int32, (CHUNK, CHUNK), 0)
    si = lax.broadcasted_iota(jnp.int32, (CHUNK, CHUNK), 1)
    return dict(
        strict=(s > t) if reverse else (s < t),
        incl=(s >= t) if reverse else (s <= t),
        eye=(s == t).astype(F32),
        bd=(rr // HEAD_DIM) == (cc // HEAD_DIM),
        first_head=lax.broadcasted_iota(jnp.int32, (CHUNK, LANES), 1) < HEAD_DIM,
        tri=((si >= ti) if reverse else (si <= ti)).astype(BF16),
    )


def _bd(x, m):
    zero = jnp.zeros((), x.dtype)
    quad = lambda h: jnp.where(m["bd"], jnp.concatenate([x[:, h * LANES:(h + 1) * LANES]] * 2, axis=0), zero)
    zq = jnp.zeros((LANES, LANES), x.dtype)
    return jnp.concatenate([jnp.concatenate([quad(0), zq], axis=0),
                            jnp.concatenate([zq, quad(1)], axis=0)], axis=1)


def _diag_blocks(full, m):
    halves = []
    for half in range(GROUP // LANES):
        lanes = slice(half * LANES, (half + 1) * LANES)
        r0 = half * LANES
        halves.append(jnp.where(m["first_head"], full[r0:r0 + CHUNK, lanes], full[r0 + CHUNK:r0 + LANES, lanes]))
    return jnp.concatenate(halves, axis=1)


def _wkv_stage0(refs, off, reverse, m):
    r_ref, v_ref, kk_ref, lw_ref, km_ref, b_ref = refs
    sl = pl.ds(off, CHUNK)
    lw = lw_ref[sl, :]
    hi, lo = _split2(lw)
    tri = m["tri"]
    cum = _dot(tri, hi) + _dot(tri, lo)
    tot = cum[0:1, :] if reverse else cum[CHUNK - 1:CHUNK, :]
    e_neg = jnp.exp(-cum)
    e_rest = jnp.exp(tot - cum)
    b = b_ref[sl, :].astype(F32)
    km = km_ref[sl, :].astype(F32)
    return dict(
        a_t=-kk_ref[sl, :].astype(F32) * jnp.exp(cum - lw),
        r_t=r_ref[sl, :].astype(F32) * jnp.exp(cum),
        b_h=b * e_neg, k_h=km * e_neg, b_r=b * e_rest, k_r=km * e_rest,
        v=v_ref[sl, :], e_tot=jnp.exp(tot),
    )


def _wkv_wave(fwd_refs, bwd_refs, off_f, off_b, mf, mb, yf_ref, yb_ref, sf_ref, sb_ref):
    pre_f = _wkv_stage0(fwd_refs, off_f, False, mf)
    pre_b = _wkv_stage0(bwd_refs, off_b, True, mb)
    yield
    insts = []
    for g in range(N_GROUPS):
        insts.append(dict(pre=pre_f, g=g, m=mf, s_ref=sf_ref, y_ref=yf_ref, sl=pl.ds(off_f, CHUNK)))
        insts.append(dict(pre=pre_b, g=g, m=mb, s_ref=sb_ref, y_ref=yb_ref, sl=pl.ds(off_b, CHUNK)))
    rng = range(len(insts))
    ms = [it["m"] for it in insts]
    ls = [slice(it["g"] * GROUP, (it["g"] + 1) * GROUP) for it in insts]
    pre = [it["pre"] for it in insts]
    rt = [pre[i]["r_t"][:, ls[i]] for i in rng]
    at_b = [pre[i]["a_t"][:, ls[i]].astype(BF16) for i in rng]
    v_b = [pre[i]["v"][:, ls[i]].astype(BF16) for i in rng]
    lhs = [jnp.concatenate([at_b[i], rt[i].astype(BF16)], axis=0) for i in rng]
    gb = [_dot_nt(lhs[i], _bd(pre[i]["b_h"][:, ls[i]].astype(BF16), ms[i])) for i in rng]
    yield
    gk = [_dot_nt(lhs[i], _bd(pre[i]["k_h"][:, ls[i]].astype(BF16), ms[i])) for i in rng]
    a_ab = [jnp.where(ms[i]["strict"], gb[i][:CHUNK], 0.0) for i in rng]
    a_rb = [jnp.where(ms[i]["incl"], gb[i][CHUNK:], 0.0).astype(BF16) for i in rng]
    yield
    t_inv = [ms[i]["eye"] + a_ab[i] for i in rng]
    xb = [a.astype(BF16) for a in a_ab]
    x = [_dot(xb[i], _bd(xb[i], ms[i])) for i in rng]
    a_k = [jnp.concatenate([jnp.where(ms[i]["strict"], gk[i][:CHUNK], 0.0),
                            jnp.where(ms[i]["incl"], gk[i][CHUNK:], 0.0)], axis=0).astype(BF16) for i in rng]
    av = [_dot(a_k[i], _bd(v_b[i], ms[i])) for i in rng]
    yield
    for _ in range(int(math.log2(CHUNK)) - 2):
        xb = [xi.astype(BF16) for xi in x]
        res = [_dot(jnp.concatenate([xb[i], t_inv[i].astype(BF16)], axis=0), _bd(xb[i], ms[i])) for i in rng]
        x = [r[:CHUNK] for r in res]
        t_inv = [t_inv[i] + res[i][CHUNK:] for i in rng]
        yield
    res = [_dot(t_inv[i].astype(BF16), _bd(x[i].astype(BF16), ms[i])) for i in rng]
    t_b = [(t_inv[i] + res[i]).astype(BF16) for i in rng]
    bk_b = [jnp.concatenate([pre[i]["b_r"][:, ls[i]], pre[i]["k_r"][:, ls[i]]], axis=0).astype(BF16) for i in rng]
    ard = [jnp.concatenate([at_b[i], rt[i].astype(BF16),
                            (ms[i]["eye"] * pre[i]["e_tot"][:, ls[i]]).astype(BF16)], axis=0) for i in rng]
    yield
    res = [_dot(ard[i], _bd(insts[i]["s_ref"][insts[i]["g"]].astype(BF16), ms[i])) for i in rng]
    yield
    u_b = [_dot(t_b[i], _bd((res[i][:CHUNK] + av[i][:CHUNK]).astype(BF16), ms[i])).astype(BF16) for i in rng]
    yield
    y = [res[i][CHUNK:2 * CHUNK] + av[i][CHUNK:] + _dot(a_rb[i], _bd(u_b[i], ms[i])) for i in rng]
    upd = [_dot_tn(bk_b[i], jnp.concatenate([u_b[i], v_b[i]], axis=0)) for i in rng]
    for i in rng:
        it = insts[i]
        it["y_ref"][it["sl"], ls[i]] = y[i].astype(it["y_ref"].dtype)
        it["s_ref"][it["g"]] = res[i][2 * CHUNK:] + _diag_blocks(upd[i], ms[i])


def _interleave(gens, lag):
    pending = list(gens)
    active = []
    tick = 0
    while pending or active:
        if pending and tick % lag == 0:
            active.append(pending.pop(0))
        for g in list(active):
            if next(g, StopIteration) is StopIteration:
                active.remove(g)
        tick += 1


def _wkv_kernel(n_chunks, *refs):
    fwd_refs = refs[0:6]
    bwd_refs = refs[6:12]
    yf_ref, yb_ref, sf_ref, sb_ref = refs[12:16]

    @pl.when(pl.program_id(1) == 0)
    def _():
        sf_ref[...] = jnp.zeros_like(sf_ref)
        sb_ref[...] = jnp.zeros_like(sb_ref)

    mf = _wkv_masks(False)
    mb = _wkv_masks(True)
    waves = [_wkv_wave(fwd_refs, bwd_refs, c * CHUNK, (n_chunks - 1 - c) * CHUNK, mf, mb,
                       yf_ref, yb_ref, sf_ref, sb_ref) for c in range(n_chunks)]
    _interleave(waves, WAVE_LAG)


def _wkv(pp, n_seq, seq_len, tb=512):
    (r, v, kk, lwf, kmf, bf, lwb, kmb, bb) = pp
    n = r.shape[0]
    nb = seq_len // tb
    fspec = pl.BlockSpec((tb, D_RWKV), lambda s, i: (s * nb + i, 0))
    bspec = pl.BlockSpec((tb, D_RWKV), lambda s, i: (s * nb + nb - 1 - i, 0))
    return pl.pallas_call(
        functools.partial(_wkv_kernel, tb // CHUNK),
        grid=(n_seq, nb),
        in_specs=[fspec] * 6 + [bspec] * 6,
        out_specs=[fspec, bspec],
        out_shape=[jax.ShapeDtypeStruct((n, D_RWKV), BF16)] * 2,
        scratch_shapes=[pltpu.VMEM((N_GROUPS, CHUNK, GROUP), F32)] * 2,
        compiler_params=_cparams(("parallel", "arbitrary")),
        name="wkv",
    )(r, v, kk, lwf, kmf, bf, r, v, kk, lwb, kmb, bb)


def _post_kernel(yf_ref, yb_ref, g_ref, bonus_ref, gnw_ref, gnb_ref, e_ref, o_ref):
    y = yf_ref[...].astype(F32) + yb_ref[...].astype(F32)
    mu = _seg_sum(y, e_ref) * (1.0 / HEAD_DIM)
    d = y - mu
    var = _seg_sum(d * d, e_ref) * (1.0 / HEAD_DIM)
    yn = d * lax.rsqrt(var + GN_EPS) * gnw_ref[...] + gnb_ref[...]
    o_ref[...] = ((yn + bonus_ref[...].astype(F32)) * g_ref[...].astype(F32)).astype(o_ref.dtype)


def _post(yf, yb, g, bonus, p, tb=512):
    n = yf.shape[0]
    spec = pl.BlockSpec((tb, D_RWKV), lambda i: (i, 0))
    vec = pl.BlockSpec((1, D_RWKV), lambda i: (0, 0))
    return pl.pallas_call(
        _post_kernel,
        grid=(n // tb,),
        in_specs=[spec] * 4 + [vec, vec, pl.BlockSpec(p["e"].shape, lambda i: (0, 0))],
        out_specs=spec,
        out_shape=jax.ShapeDtypeStruct((n, D_RWKV), BF16),
        compiler_params=_cparams(("parallel",)),
        name="rwkv_post",
    )(yf, yb, g, bonus, p["gn_w"], p["gn_b"], p["e"])


CONV_ROWS = 64


def _conv_kernel(seq_len, tb, u_ref, up_ref, un_ref, dww_ref, dwb_ref, lnw_ref, lnb_ref,
                 o_ref, u_scr, sh_scr, c_scr):
    i = pl.program_id(0)
    t0 = (i * tb) % seq_len
    first = t0 == 0
    last = (t0 + tb) == seq_len

    u_scr[HALO:HALO + tb, :] = u_ref[...].astype(F32)
    u_scr[0:HALO, :] = jnp.where(first, 0.0, up_ref[...].astype(F32))
    u_scr[HALO + tb:2 * HALO + tb, :] = jnp.where(last, 0.0, un_ref[...].astype(F32))

    sh_rows = tb + 2 * HALO - SUBLANES
    for s in range(1, SUBLANES):
        sh_scr[s - 1] = u_scr[s:s + sh_rows, :]

    base = HALO - CONV_HALF
    for cb in range(D_CONV // LANES):
        cs = slice(cb * LANES, (cb + 1) * LANES)
        wcol = dww_ref[:, cs]
        bias = dwb_ref[:, cs]
        for rc in range(tb // CONV_ROWS):
            r0 = rc * CONV_ROWS
            acc = jnp.zeros((CONV_ROWS, LANES), F32) + bias
            for j in range(CONV_WIDTH):
                s = (base + j) % SUBLANES
                q = r0 + base + j - s
                if s == 0:
                    u = u_scr[q:q + CONV_ROWS, cs]
                else:
                    u = sh_scr[s - 1, q:q + CONV_ROWS, cs]
                acc = acc + u * wcol[j:j + 1, :]
            c_scr[r0:r0 + CONV_ROWS, cs] = acc

    c = c_scr[...]
    cm = jnp.mean(c, axis=-1, keepdims=True)
    d = c - cm
    cv = jnp.mean(d * d, axis=-1, keepdims=True)
    y = d * lax.rsqrt(cv + LN_EPS) * lnw_ref[...] + lnb_ref[...]
    o_ref[...] = (y * jax.nn.sigmoid(y)).astype(o_ref.dtype)


def _conv(u, seq_len, p, tb=256):
    n = u.shape[0]
    hb = tb // HALO
    nhb = n // HALO
    vec = pl.BlockSpec((1, D_CONV), lambda i: (0, 0))
    return pl.pallas_call(
        functools.partial(_conv_kernel, seq_len, tb),
        grid=(n // tb,),
        in_specs=[
            pl.BlockSpec((tb, D_CONV), lambda i: (i, 0)),
            pl.BlockSpec((HALO, D_CONV), lambda i: (jnp.maximum(i * hb - 1, 0), 0)),
            pl.BlockSpec((HALO, D_CONV), lambda i: (jnp.minimum((i + 1) * hb, nhb - 1), 0)),
            pl.BlockSpec((CONV_WIDTH, D_CONV), lambda i: (0, 0)),
            vec, vec, vec,
        ],
        out_specs=pl.BlockSpec((tb, D_CONV), lambda i: (i, 0)),
        out_shape=jax.ShapeDtypeStruct((n, D_CONV), BF16),
        scratch_shapes=[
            pltpu.VMEM((tb + 2 * HALO, D_CONV), F32),
            pltpu.VMEM((SUBLANES - 1, tb + 2 * HALO - SUBLANES, D_CONV), F32),
            pltpu.VMEM((tb, D_CONV), F32),
        ],
        compiler_params=_cparams(("parallel",)),
        name="conv_module",
    )(u, u, u, p["dw_w"], p["dw_b"], p["cln_w"], p["cln_b"])


OUTPROJ_ROWS = 256


def _outproj_kernel(n_lo, orw_ref, ocv_ref, xp_ref, xs_ref, w_ref, gpost_ref, gpre_ref, x1_ref, hm_ref):
    from_prompt = pl.program_id(0) < n_lo
    tm = x1_ref.shape[0]
    for r0 in range(0, tm, OUTPROJ_ROWS):
        rs = slice(r0, r0 + OUTPROJ_ROWS)
        mix = _dot(orw_ref[rs, :], w_ref[0:D_RWKV, :]) + _dot(ocv_ref[rs, :], w_ref[D_RWKV:D_MODEL, :])
        x = jnp.where(from_prompt, xp_ref[rs, :], xs_ref[rs, :])
        x1 = x + _rms_rows(mix, gpost_ref[...])
        x1_ref[rs, :] = x1
        hm_ref[rs, :] = _rms_rows(x1, gpre_ref[...]).astype(hm_ref.dtype)


def _outproj(o_rwkv, o_conv, xp, xs, p, tm=512):
    n = o_rwkv.shape[0]
    n_lo = xp.shape[0] // tm
    half = pl.BlockSpec((tm, D_RWKV), lambda i: (i, 0))
    row = pl.BlockSpec((tm, D_MODEL), lambda i: (i, 0))
    vec = pl.BlockSpec((1, D_MODEL), lambda i: (0, 0))
    return pl.pallas_call(
        functools.partial(_outproj_kernel, n_lo),
        grid=(n // tm,),
        in_specs=[half, half, _lo_spec(tm, D_MODEL, n_lo), _hi_spec(tm, D_MODEL, n_lo),
                  pl.BlockSpec((D_MODEL, D_MODEL), lambda i: (0, 0)), vec, vec],
        out_specs=[row, row],
        out_shape=[jax.ShapeDtypeStruct((n, D_MODEL), F32), jax.ShapeDtypeStruct((n, D_MODEL), BF16)],
        compiler_params=_cparams(("parallel",)),
        name="outproj",
    )(o_rwkv, o_conv, xp, xs, p["w_out"], p["g_post_mix"], p["g_pre_mlp"])


def _mlp_kernel(n_lo, hm_ref, x1_ref, wup_ref, wdn_ref, g_ref, op_ref, os_ref, acc_ref):
    i = pl.program_id(0)
    j = pl.program_id(1)
    @pl.when(j == 0)
    def _():
        acc_ref[...] = jnp.zeros_like(acc_ref)

    h = _dot(hm_ref[...], wup_ref[...])
    h = jnp.square(jnp.maximum(h, 0.0)).astype(BF16)
    acc_ref[...] += _dot(h, wdn_ref[...])

    def finish(o_ref):
        o_ref[...] = x1_ref[...] + _rms_rows(acc_ref[...], g_ref[...])

    is_last = j == pl.num_programs(1) - 1
    pl.when(is_last & (i < n_lo))(lambda: finish(op_ref))
    pl.when(is_last & (i >= n_lo))(lambda: finish(os_ref))


def _mlp(hm, x1, n_prompt, p, tm=512, tf=1024):
    n = hm.shape[0]
    n_lo = n_prompt // tm
    row = pl.BlockSpec((tm, D_MODEL), lambda i, j: (i, 0))
    return pl.pallas_call(
        functools.partial(_mlp_kernel, n_lo),
        grid=(n // tm, D_FF // tf),
        in_specs=[
            row, row,
            pl.BlockSpec((D_MODEL, tf), lambda i, j: (0, j)),
            pl.BlockSpec((tf, D_MODEL), lambda i, j: (j, 0)),
            pl.BlockSpec((1, D_MODEL), lambda i, j: (0, 0)),
        ],
        out_specs=[_lo_spec(tm, D_MODEL, n_lo), _hi_spec(tm, D_MODEL, n_lo)],
        out_shape=[jax.ShapeDtypeStruct((n_prompt, D_MODEL), F32),
                   jax.ShapeDtypeStruct((n - n_prompt, D_MODEL), F32)],
        scratch_shapes=[pltpu.VMEM((tm, D_MODEL), F32)],
        compiler_params=_cparams(("arbitrary", "arbitrary")),
        name="mlp",
    )(hm, x1, p["w_up"], p["w_down"], p["g_post_mlp"])


def _pad_rows(w, rows, offset):
    out = jnp.zeros((rows, w.shape[1]), w.dtype)
    return out.at[offset:offset + w.shape[0]].set(w)


def _layer_params(l, g_pre_mix, w_in, mu_prev, mu_next, w0_f, w2_f, w0_b, w2_b, a0_f, a2_f, a0_b, a2_b,
                  g2, k_k, k_a, r_k, gn_w, gn_b, dw_w, dw_b, cln_w, cln_b, w_out, g_post_mix,
                  g_pre_mlp, w_up, w_down, g_post_mlp):
    rwkv_cols = 3 * D_RWKV + LORA_COLS
    row = lambda a: a[l].reshape(1, -1).astype(F32)
    pad_cols = lambda a, width: jnp.pad(a, ((0, 0), (0, width - a.shape[1])))
    w = (g_pre_mix[l].astype(F32)[:, None] * w_in[l]).astype(BF16)
    heads = jnp.arange(D_RWKV) // HEAD_DIM
    return dict(
        w_zr=pad_cols(w[:, :rwkv_cols], ZR_COLS), w_zc=w[:, rwkv_cols:],
        mu_prev=pad_cols(row(mu_prev), ZR_COLS), mu_next=pad_cols(row(mu_next), ZR_COLS),
        w2=jnp.concatenate([_pad_rows(w2_f[l], 128, 0), _pad_rows(w2_b[l], 128, DECAY_LORA)], axis=1).astype(BF16),
        a2=jnp.concatenate([_pad_rows(a2_f[l], 128, 0), _pad_rows(a2_b[l], 128, AAA_LORA)], axis=1).astype(BF16),
        g2=_pad_rows(g2[l], 256, 0).astype(BF16),
        w0=jnp.concatenate([row(w0_f), row(w0_b)], axis=1),
        a0=jnp.concatenate([row(a0_f), row(a0_b)], axis=1),
        k_k=row(k_k), k_a=row(k_a), r_k=row(r_k), gn_w=row(gn_w), gn_b=row(gn_b),
        e=(heads[:, None] == heads[None, :]).astype(BF16),
        dw_w=dw_w[l].astype(F32), dw_b=row(dw_b), cln_w=row(cln_w), cln_b=row(cln_b),
        w_out=w_out[l].astype(BF16), g_post_mix=row(g_post_mix), g_pre_mlp=row(g_pre_mlp),
        w_up=w_up[l].astype(BF16), w_down=w_down[l].astype(BF16), g_post_mlp=row(g_post_mlp),
    )


def _layer(xp, xs, seq_len, p):
    n_seq = (xp.shape[0] + xs.shape[0]) // seq_len
    zr = _inproj(xp, xs, p["w_zr"], "inproj_rwkv")
    u_glu = _inproj(xp, xs, p["w_zc"], "inproj_conv", glu=True)
    prep = _prep(zr, seq_len, p)
    g, bonus = prep[9], prep[10]
    yf, yb = _wkv(prep[:9], n_seq, seq_len)
    o_rwkv = _post(yf, yb, g, bonus, p)
    o_conv = _conv(u_glu, seq_len, p)
    x1, hm = _outproj(o_rwkv, o_conv, xp, xs, p)
    return _mlp(hm, x1, xp.shape[0], p)


def _forward(x_prompt, x_sample, weights):
    seq_len = x_prompt.shape[1]
    assert x_sample.shape[1] == seq_len
    xp = x_prompt.reshape(-1, D_MODEL)
    xs = x_sample.reshape(-1, D_MODEL)
    depth = weights[0].shape[0]
    for l in range(depth):
        xp, xs = _layer(xp, xs, seq_len, _layer_params(l, *weights))
    return xp.reshape(x_prompt.shape), xs.reshape(x_sample.shape)


def kernel(x_prompt, x_sample, g_pre_mix, w_in, mu_prev, mu_next, w0_f, w2_f, w0_b, w2_b, a0_f, a2_f, a0_b, a2_b, g2, k_k, k_a, r_k, gn_w, gn_b, dw_w, dw_b, cln_w, cln_b, w_out, g_post_mix, g_pre_mlp, w_up, w_down, g_post_mlp):
    weights = (g_pre_mix, w_in, mu_prev, mu_next, w0_f, w2_f, w0_b, w2_b, a0_f, a2_f, a0_b, a2_b,
               g2, k_k, k_a, r_k, gn_w, gn_b, dw_w, dw_b, cln_w, cln_b, w_out, g_post_mix,
               g_pre_mlp, w_up, w_down, g_post_mlp)
    return _forward(x_prompt, x_sample, weights)
```

```python
import functools
import math

import jax
import jax.numpy as jnp
from jax import lax
from jax.experimental import pallas as pl
from jax.experimental.pallas import tpu as pltpu

F32 = jnp.float32
BF16 = jnp.bfloat16

D_MODEL = 2048
D_RWKV = 1024
D_CONV = 1024
HEAD_DIM = 64
N_HEADS = 16
DECAY_LORA = 64
AAA_LORA = 64
GATE_LORA = 160
CONV_WIDTH = 31
CONV_HALF = CONV_WIDTH // 2
D_FF = 4 * D_MODEL
RMS_EPS = 1e-6
LN_EPS = 1e-5
GN_EPS = 64e-5
L2_EPS = 1e-12

LORA_COLS = 2 * DECAY_LORA + 2 * AAA_LORA + GATE_LORA
LORA_PAD = 512
ZR_COLS = 3 * D_RWKV + LORA_PAD

CHUNK = 64
GROUP = 256
N_GROUPS = D_RWKV // GROUP
WAVE_LAG = 3
HALO = 16
SUBLANES = 8
LANES = 128

VMEM_LIMIT = 56 * 1024 * 1024


def _cparams(sem):
    return pltpu.CompilerParams(dimension_semantics=sem, vmem_limit_bytes=VMEM_LIMIT)


def _dot(a, b):
    return jnp.dot(a, b, preferred_element_type=F32)


def _dot_nt(a, b):
    return lax.dot_general(a, b, (((1,), (1,)), ((), ())), preferred_element_type=F32)


def _dot_tn(a, b):
    return lax.dot_general(a, b, (((0,), (0,)), ((), ())), preferred_element_type=F32)


def _split2(x):
    hi = x.astype(BF16)
    lo = (x - hi.astype(F32)).astype(BF16)
    return hi, lo


def _seg_sum(x, e_ref):
    return _dot(x.astype(BF16), e_ref[...])


def _rms_rows(x, g):
    ms = jnp.mean(x * x, axis=-1, keepdims=True)
    return x * lax.rsqrt(ms + RMS_EPS) * g


def _lo_spec(tm, width, n_lo):
    return pl.BlockSpec((tm, width), lambda i, *_: (jnp.minimum(i, n_lo - 1), 0))


def _hi_spec(tm, width, n_lo):
    return pl.BlockSpec((tm, width), lambda i, *_: (jnp.maximum(i - n_lo, 0), 0))


def _inproj_kernel(n_lo, tn, glu, xp_ref, xs_ref, w_ref, z_ref):
    def run(x_ref):
        x = x_ref[...]
        rinv = lax.rsqrt(jnp.mean(x * x, axis=-1, keepdims=True) + RMS_EPS)
        xb = x.astype(BF16)
        out_cols = z_ref.shape[1]
        for c in range(out_cols // tn):
            cs = slice(c * tn, (c + 1) * tn)
            z = _dot(xb, w_ref[:, cs]) * rinv
            if glu:
                gate = _dot(xb, w_ref[:, out_cols + c * tn:out_cols + (c + 1) * tn]) * rinv
                z = z * jax.nn.sigmoid(gate)
            z_ref[:, cs] = z.astype(z_ref.dtype)

    i = pl.program_id(0)
    pl.when(i < n_lo)(lambda: run(xp_ref))
    pl.when(i >= n_lo)(lambda: run(xs_ref))


def _inproj(xp, xs, w, name, glu=False, tm=512, tn=512):
    n_lo = xp.shape[0] // tm
    n = xp.shape[0] + xs.shape[0]
    cols = w.shape[1]
    out_cols = cols // 2 if glu else cols
    return pl.pallas_call(
        functools.partial(_inproj_kernel, n_lo, tn, glu),
        grid=(n // tm,),
        in_specs=[
            _lo_spec(tm, D_MODEL, n_lo),
            _hi_spec(tm, D_MODEL, n_lo),
            pl.BlockSpec((D_MODEL, cols), lambda i: (0, 0), pipeline_mode=pl.Buffered(1)),
        ],
        out_specs=pl.BlockSpec((tm, out_cols), lambda i: (i, 0)),
        out_shape=jax.ShapeDtypeStruct((n, out_cols), BF16),
        compiler_params=_cparams(("parallel",)),
        name=name,
    )(xp, xs, w)


def _prep_kernel(seq_len, tb,
                 z_ref, zp_ref, zn_ref, mup_ref, mun_ref, w2_ref, a2_ref, g2_ref,
                 w0_ref, a0_ref, kk_w_ref, ka_ref, rk_ref, e_ref,
                 r_ref, v_ref, kk_ref, lwf_ref, kmf_ref, bf_ref, lwb_ref, kmb_ref, bb_ref,
                 g_ref, bonus_ref):
    i = pl.program_id(0)
    t0 = (i * tb) % seq_len
    first = t0 == 0
    last = (t0 + tb) == seq_len
    edge = lax.broadcasted_iota(jnp.int32, (SUBLANES, 1), 0)

    def shifted(c0, c1):
        z = z_ref[:, c0:c1].astype(F32)
        mp = mup_ref[:, c0:c1]
        mn = mun_ref[:, c0:c1]
        body = (1.0 - mp - mn) * z + mp * pltpu.roll(z, 1, axis=0) + mn * pltpu.roll(z, tb - 1, axis=0)
        prev_row = jnp.where(first, 0.0, zp_ref[HALO - 1:HALO, c0:c1].astype(F32))
        next_row = jnp.where(last, 0.0, zn_ref[0:1, c0:c1].astype(F32))
        top = body[0:SUBLANES] + jnp.where(edge == 0, mp * (prev_row - z[tb - 1:tb]), 0.0)
        bot = body[tb - SUBLANES:tb] + jnp.where(edge == SUBLANES - 1, mn * (next_row - z[0:1]), 0.0)
        return jnp.concatenate([top, body[SUBLANES:tb - SUBLANES], bot], axis=0)

    r = shifted(0, D_RWKV)
    k = shifted(D_RWKV, 2 * D_RWKV)
    v = shifted(2 * D_RWKV, 3 * D_RWKV)
    lora = shifted(3 * D_RWKV, ZR_COLS)

    xw = jnp.tanh(lora[:, 0:128]).astype(BF16)
    xa = lora[:, 128:256].astype(BF16)
    xg = jax.nn.sigmoid(lora[:, 256:512]).astype(BF16)
    dw = _dot(xw, w2_ref[...]) + w0_ref[...]
    da = _dot(xa, a2_ref[...]) + a0_ref[...]
    g_ref[...] = _dot(xg, g2_ref[...]).astype(g_ref.dtype)

    kraw = k * kk_w_ref[...]
    ss = _seg_sum(kraw * kraw, e_ref)
    kk = kraw * lax.rsqrt(jnp.maximum(ss, L2_EPS * L2_EPS))
    rk = _seg_sum(r * k * rk_ref[...], e_ref)
    bonus_ref[...] = (rk * v).astype(bonus_ref.dtype)
    r_ref[...] = r.astype(r_ref.dtype)
    v_ref[...] = v.astype(v_ref.dtype)
    kk_ref[...] = kk.astype(kk_ref.dtype)

    neg_c = -math.exp(-0.5)
    ka = ka_ref[...]
    for d, (lw_ref, km_ref, b_ref) in enumerate(((lwf_ref, kmf_ref, bf_ref), (lwb_ref, kmb_ref, bb_ref))):
        sl = slice(d * D_RWKV, (d + 1) * D_RWKV)
        lw_ref[...] = neg_c * jax.nn.sigmoid(dw[:, sl])
        a = jax.nn.sigmoid(da[:, sl])
        km_ref[...] = (k * (1.0 + (a - 1.0) * ka)).astype(km_ref.dtype)
        b_ref[...] = (kk * a).astype(b_ref.dtype)


_PREP_OUT_DTYPES = (BF16, BF16, BF16, F32, BF16, BF16, F32, BF16, BF16, BF16, BF16)


def _prep(zr, seq_len, p, tb=256):
    n = zr.shape[0]
    hb = tb // HALO
    nhb = n // HALO
    vec = lambda w: pl.BlockSpec((1, w), lambda i: (0, 0))
    full = lambda a: pl.BlockSpec(a.shape, lambda i: (0, 0))
    out_spec = pl.BlockSpec((tb, D_RWKV), lambda i: (i, 0))
    return pl.pallas_call(
        functools.partial(_prep_kernel, seq_len, tb),
        grid=(n // tb,),
        in_specs=[
            pl.BlockSpec((tb, ZR_COLS), lambda i: (i, 0)),
            pl.BlockSpec((HALO, ZR_COLS), lambda i: (jnp.maximum(i * hb - 1, 0), 0)),
            pl.BlockSpec((HALO, ZR_COLS), lambda i: (jnp.minimum((i + 1) * hb, nhb - 1), 0)),
            vec(ZR_COLS), vec(ZR_COLS),
            full(p["w2"]), full(p["a2"]), full(p["g2"]),
            vec(2 * D_RWKV), vec(2 * D_RWKV), vec(D_RWKV), vec(D_RWKV), vec(D_RWKV),
            full(p["e"]),
        ],
        out_specs=[out_spec] * 11,
        out_shape=[jax.ShapeDtypeStruct((n, D_RWKV), dt) for dt in _PREP_OUT_DTYPES],
        compiler_params=_cparams(("parallel",)),
        name="rwkv_prep",
    )(zr, zr, zr, p["mu_prev"], p["mu_next"], p["w2"], p["a2"], p["g2"],
      p["w0"], p["a0"], p["k_k"], p["k_a"], p["r_k"], p["e"])


def _wkv_masks(reverse):
    t = lax.broadcasted_iota(jnp.int32, (CHUNK, GROUP), 0)
    lane = lax.broadcasted_iota(jnp.int32, (CHUNK, GROUP), 1)
    s = lane % CHUNK
    rr = lax.broadcasted_iota(jnp.int32, (LANES, LANES), 0)
    cc = lax.broadcasted_iota(jnp.int32, (LANES, LANES), 1)
    ti = lax.broadcasted_iota(jnp.int32, (CHUNK, CHUNK), 0)
    si = lax.broadcasted_iota(jnp.int32, (CHUNK, CHUNK), 1)
    return dict(
        strict=(s > t) if reverse else (s < t),
        incl=(s >= t) if reverse else (s <= t),
        eye=(s == t).astype(F32),
        bd=(rr // HEAD_DIM) == (cc // HEAD_DIM),
        first_head=lax.broadcasted_iota(jnp.int32, (CHUNK, LANES), 1) < HEAD_DIM,
        first_head_t=lax.broadcasted_iota(jnp.int32, (GROUP, LANES), 1) < HEAD_DIM,
        tri=((si >= ti) if reverse else (si <= ti)).astype(BF16),
    )


def _bd(x, m):
    zero = jnp.zeros((), x.dtype)
    quad = lambda h: jnp.where(m["bd"], jnp.concatenate([x[:, h * LANES:(h + 1) * LANES]] * 2, axis=0), zero)
    zq = jnp.zeros((LANES, LANES), x.dtype)
    return jnp.concatenate([jnp.concatenate([quad(0), zq], axis=0),
                            jnp.concatenate([zq, quad(1)], axis=0)], axis=1)


def _bd_t(x_t, m):
    zero = jnp.zeros((), x_t.dtype)
    zq = jnp.zeros((LANES, LANES), x_t.dtype)
    return jnp.concatenate([jnp.concatenate([jnp.where(m["bd"], x_t[0:LANES], zero), zq], axis=0),
                            jnp.concatenate([zq, jnp.where(m["bd"], x_t[LANES:2 * LANES], zero)], axis=0)], axis=1)


def _diag_blocks(full, m):
    halves = []
    for half in range(GROUP // LANES):
        lanes = slice(half * LANES, (half + 1) * LANES)
        r0 = half * LANES
        halves.append(jnp.where(m["first_head"], full[r0:r0 + CHUNK, lanes], full[r0 + CHUNK:r0 + LANES, lanes]))
    return jnp.concatenate(halves, axis=1)


def _wkv_stage0(refs, off, reverse, m):
    r_ref, v_ref, kk_ref, lw_ref, km_ref, b_ref = refs
    sl = pl.ds(off, CHUNK)
    lw = lw_ref[sl, :]
    hi, lo = _split2(lw)
    tri = m["tri"]
    cum = _dot(tri, hi) + _dot(tri, lo)
    tot = cum[0:1, :] if reverse else cum[CHUNK - 1:CHUNK, :]
    e_neg = jnp.exp(-cum)
    e_rest = jnp.exp(tot - cum)
    b = b_ref[sl, :].astype(F32)
    km = km_ref[sl, :].astype(F32)
    return dict(
        a_t=-kk_ref[sl, :].astype(F32) * jnp.exp(cum - lw),
        r_t=r_ref[sl, :].astype(F32) * jnp.exp(cum),
        b_h=b * e_neg, k_h=km * e_neg, b_r=b * e_rest, k_r=km * e_rest,
        v=v_ref[sl, :], e_tot=jnp.exp(tot),
    )


def _wkv_wave(fwd_refs, bwd_refs, off_f, off_b, mf, mb, yf_ref, yb_ref, sf_ref, sb_ref):
    pre_f = _wkv_stage0(fwd_refs, off_f, False, mf)
    pre_b = _wkv_stage0(bwd_refs, off_b, True, mb)
    yield
    insts = []
    for g in range(N_GROUPS):
        insts.append(dict(pre=pre_f, g=g, m=mf, s_ref=sf_ref, y_ref=yf_ref, sl=pl.ds(off_f, CHUNK)))
        insts.append(dict(pre=pre_b, g=g, m=mb, s_ref=sb_ref, y_ref=yb_ref, sl=pl.ds(off_b, CHUNK)))
    rng = range(len(insts))
    ms = [it["m"] for it in insts]
    ls = [slice(it["g"] * GROUP, (it["g"] + 1) * GROUP) for it in insts]
    pre = [it["pre"] for it in insts]
    rt = [pre[i]["r_t"][:, ls[i]] for i in rng]
    at_b = [pre[i]["a_t"][:, ls[i]].astype(BF16) for i in rng]
    v_b = [pre[i]["v"][:, ls[i]].astype(BF16) for i in rng]
    lhs = [jnp.concatenate([at_b[i], rt[i].astype(BF16)], axis=0) for i in rng]
    bk_t = [jnp.concatenate([pre[i]["b_h"][:, ls[i]], pre[i]["k_h"][:, ls[i]]], axis=0).T for i in rng]
    bk_s = [pltpu.roll(t, HEAD_DIM, axis=1) for t in bk_t]
    gb = [_dot(lhs[i], _bd_t(jnp.where(ms[i]["first_head_t"], bk_t[i], bk_s[i]).astype(BF16), ms[i])) for i in rng]
    yield
    gk = [_dot(lhs[i], _bd_t(jnp.where(ms[i]["first_head_t"], bk_s[i], bk_t[i]).astype(BF16), ms[i])) for i in rng]
    a_ab = [jnp.where(ms[i]["strict"], gb[i][:CHUNK], 0.0) for i in rng]
    a_rb = [jnp.where(ms[i]["incl"], gb[i][CHUNK:], 0.0).astype(BF16) for i in rng]
    yield
    t_inv = [ms[i]["eye"] + a_ab[i] for i in rng]
    xb = [a.astype(BF16) for a in a_ab]
    x = [_dot(xb[i], _bd(xb[i], ms[i])) for i in rng]
    a_k = [jnp.concatenate([jnp.where(ms[i]["strict"], gk[i][:CHUNK], 0.0),
                            jnp.where(ms[i]["incl"], gk[i][CHUNK:], 0.0)], axis=0).astype(BF16) for i in rng]
    av = [_dot(a_k[i], _bd(v_b[i], ms[i])) for i in rng]
    yield
    for _ in range(int(math.log2(CHUNK)) - 2):
        xb = [xi.astype(BF16) for xi in x]
        res = [_dot(jnp.concatenate([xb[i], t_inv[i].astype(BF16)], axis=0), _bd(xb[i], ms[i])) for i in rng]
        x = [r[:CHUNK] for r in res]
        t_inv = [t_inv[i] + res[i][CHUNK:] for i in rng]
        yield
    res = [_dot(t_inv[i].astype(BF16), _bd(x[i].astype(BF16), ms[i])) for i in rng]
    t_b = [(t_inv[i] + res[i]).astype(BF16) for i in rng]
    bk_b = [jnp.concatenate([pre[i]["b_r"][:, ls[i]], pre[i]["k_r"][:, ls[i]]], axis=0).astype(BF16) for i in rng]
    ard = [jnp.concatenate([at_b[i], rt[i].astype(BF16),
                            (ms[i]["eye"] * pre[i]["e_tot"][:, ls[i]]).astype(BF16)], axis=0) for i in rng]
    yield
    res = [_dot(ard[i], _bd(insts[i]["s_ref"][insts[i]["g"]].astype(BF16), ms[i])) for i in rng]
    yield
    u_b = [_dot(t_b[i], _bd((res[i][:CHUNK] + av[i][:CHUNK]).astype(BF16), ms[i])).astype(BF16) for i in rng]
    yield
    y = [res[i][CHUNK:2 * CHUNK] + av[i][CHUNK:] + _dot(a_rb[i], _bd(u_b[i], ms[i])) for i in rng]
    upd = [_dot_tn(bk_b[i], jnp.concatenate([u_b[i], v_b[i]], axis=0)) for i in rng]
    for i in rng:
        it = insts[i]
        it["y_ref"][it["sl"], ls[i]] = y[i].astype(it["y_ref"].dtype)
        it["s_ref"][it["g"]] = res[i][2 * CHUNK:] + _diag_blocks(upd[i], ms[i])


def _interleave(gens, lag):
    pending = list(gens)
    active = []
    tick = 0
    while pending or active:
        if pending and tick % lag == 0:
            active.append(pending.pop(0))
        for g in list(active):
            if next(g, StopIteration) is StopIteration:
                active.remove(g)
        tick += 1


def _wkv_kernel(n_chunks, *refs):
    fwd_refs = refs[0:6]
    bwd_refs = refs[6:12]
    yf_ref, yb_ref, sf_ref, sb_ref = refs[12:16]

    @pl.when(pl.program_id(1) == 0)
    def _():
        sf_ref[...] = jnp.zeros_like(sf_ref)
        sb_ref[...] = jnp.zeros_like(sb_ref)

    mf = _wkv_masks(False)
    mb = _wkv_masks(True)
    waves = [_wkv_wave(fwd_refs, bwd_refs, c * CHUNK, (n_chunks - 1 - c) * CHUNK, mf, mb,
                       yf_ref, yb_ref, sf_ref, sb_ref) for c in range(n_chunks)]
    _interleave(waves, WAVE_LAG)


def _wkv(pp, n_seq, seq_len, tb=512):
    (r, v, kk, lwf, kmf, bf, lwb, kmb, bb) = pp
    n = r.shape[0]
    nb = seq_len // tb
    fspec = pl.BlockSpec((tb, D_RWKV), lambda s, i: (s * nb + i, 0))
    bspec = pl.BlockSpec((tb, D_RWKV), lambda s, i: (s * nb + nb - 1 - i, 0))
    return pl.pallas_call(
        functools.partial(_wkv_kernel, tb // CHUNK),
        grid=(n_seq, nb),
        in_specs=[fspec] * 6 + [bspec] * 6,
        out_specs=[fspec, bspec],
        out_shape=[jax.ShapeDtypeStruct((n, D_RWKV), BF16)] * 2,
        scratch_shapes=[pltpu.VMEM((N_GROUPS, CHUNK, GROUP), F32)] * 2,
        compiler_params=_cparams(("parallel", "arbitrary")),
        name="wkv",
    )(r, v, kk, lwf, kmf, bf, r, v, kk, lwb, kmb, bb)


def _post_kernel(yf_ref, yb_ref, g_ref, bonus_ref, gnw_ref, gnb_ref, e_ref, o_ref):
    y = yf_ref[...].astype(F32) + yb_ref[...].astype(F32)
    mu = _seg_sum(y, e_ref) * (1.0 / HEAD_DIM)
    d = y - mu
    var = _seg_sum(d * d, e_ref) * (1.0 / HEAD_DIM)
    yn = d * lax.rsqrt(var + GN_EPS) * gnw_ref[...] + gnb_ref[...]
    o_ref[...] = ((yn + bonus_ref[...].astype(F32)) * g_ref[...].astype(F32)).astype(o_ref.dtype)


def _post(yf, yb, g, bonus, p, tb=512):
    n = yf.shape[0]
    spec = pl.BlockSpec((tb, D_RWKV), lambda i: (i, 0))
    vec = pl.BlockSpec((1, D_RWKV), lambda i: (0, 0))
    return pl.pallas_call(
        _post_kernel,
        grid=(n // tb,),
        in_specs=[spec] * 4 + [vec, vec, pl.BlockSpec(p["e"].shape, lambda i: (0, 0))],
        out_specs=spec,
        out_shape=jax.ShapeDtypeStruct((n, D_RWKV), BF16),
        compiler_params=_cparams(("parallel",)),
        name="rwkv_post",
    )(yf, yb, g, bonus, p["gn_w"], p["gn_b"], p["e"])


CONV_ROWS = 64


def _conv_kernel(seq_len, tb, u_ref, up_ref, un_ref, dww_ref, dwb_ref, lnw_ref, lnb_ref,
                 o_ref, u_scr, sh_scr, c_scr):
    i = pl.program_id(0)
    t0 = (i * tb) % seq_len
    first = t0 == 0
    last = (t0 + tb) == seq_len

    u_scr[HALO:HALO + tb, :] = u_ref[...].astype(F32)
    u_scr[0:HALO, :] = jnp.where(first, 0.0, up_ref[...].astype(F32))
    u_scr[HALO + tb:2 * HALO + tb, :] = jnp.where(last, 0.0, un_ref[...].astype(F32))

    sh_rows = tb + 2 * HALO - SUBLANES
    for s in range(1, SUBLANES):
        sh_scr[s - 1] = u_scr[s:s + sh_rows, :]

    base = HALO - CONV_HALF
    for cb in range(D_CONV // LANES):
        cs = slice(cb * LANES, (cb + 1) * LANES)
        wcol = dww_ref[:, cs]
        bias = dwb_ref[:, cs]
        for rc in range(tb // CONV_ROWS):
            r0 = rc * CONV_ROWS
            acc = jnp.zeros((CONV_ROWS, LANES), F32) + bias
            for j in range(CONV_WIDTH):
                s = (base + j) % SUBLANES
                q = r0 + base + j - s
                if s == 0:
                    u = u_scr[q:q + CONV_ROWS, cs]
                else:
                    u = sh_scr[s - 1, q:q + CONV_ROWS, cs]
                acc = acc + u * wcol[j:j + 1, :]
            c_scr[r0:r0 + CONV_ROWS, cs] = acc

    c = c_scr[...]
    cm = jnp.mean(c, axis=-1, keepdims=True)
    d = c - cm
    cv = jnp.mean(d * d, axis=-1, keepdims=True)
    y = d * lax.rsqrt(cv + LN_EPS) * lnw_ref[...] + lnb_ref[...]
    o_ref[...] = (y * jax.nn.sigmoid(y)).astype(o_ref.dtype)


def _conv(u, seq_len, p, tb=256):
    n = u.shape[0]
    hb = tb // HALO
    nhb = n // HALO
    vec = pl.BlockSpec((1, D_CONV), lambda i: (0, 0))
    return pl.pallas_call(
        functools.partial(_conv_kernel, seq_len, tb),
        grid=(n // tb,),
        in_specs=[
            pl.BlockSpec((tb, D_CONV), lambda i: (i, 0)),
            pl.BlockSpec((HALO, D_CONV), lambda i: (jnp.maximum(i * hb - 1, 0), 0)),
            pl.BlockSpec((HALO, D_CONV), lambda i: (jnp.minimum((i + 1) * hb, nhb - 1), 0)),
            pl.BlockSpec((CONV_WIDTH, D_CONV), lambda i: (0, 0)),
            vec, vec, vec,
        ],
        out_specs=pl.BlockSpec((tb, D_CONV), lambda i: (i, 0)),
        out_shape=jax.ShapeDtypeStruct((n, D_CONV), BF16),
        scratch_shapes=[
            pltpu.VMEM((tb + 2 * HALO, D_CONV), F32),
            pltpu.VMEM((SUBLANES - 1, tb + 2 * HALO - SUBLANES, D_CONV), F32),
            pltpu.VMEM((tb, D_CONV), F32),
        ],
        compiler_params=_cparams(("parallel",)),
        name="conv_module",
    )(u, u, u, p["dw_w"], p["dw_b"], p["cln_w"], p["cln_b"])


OUTPROJ_ROWS = 256


def _outproj_kernel(n_lo, orw_ref, ocv_ref, xp_ref, xs_ref, w_ref, gpost_ref, gpre_ref, x1_ref, hm_ref):
    from_prompt = pl.program_id(0) < n_lo
    tm = x1_ref.shape[0]
    for r0 in range(0, tm, OUTPROJ_ROWS):
        rs = slice(r0, r0 + OUTPROJ_ROWS)
        mix = _dot(orw_ref[rs, :], w_ref[0:D_RWKV, :]) + _dot(ocv_ref[rs, :], w_ref[D_RWKV:D_MODEL, :])
        x = jnp.where(from_prompt, xp_ref[rs, :], xs_ref[rs, :])
        x1 = x + _rms_rows(mix, gpost_ref[...])
        x1_ref[rs, :] = x1
        hm_ref[rs, :] = _rms_rows(x1, gpre_ref[...]).astype(hm_ref.dtype)


def _outproj(o_rwkv, o_conv, xp, xs, p, tm=512):
    n = o_rwkv.shape[0]
    n_lo = xp.shape[0] // tm
    half = pl.BlockSpec((tm, D_RWKV), lambda i: (i, 0))
    row = pl.BlockSpec((tm, D_MODEL), lambda i: (i, 0))
    vec = pl.BlockSpec((1, D_MODEL), lambda i: (0, 0))
    return pl.pallas_call(
        functools.partial(_outproj_kernel, n_lo),
        grid=(n // tm,),
        in_specs=[half, half, _lo_spec(tm, D_MODEL, n_lo), _hi_spec(tm, D_MODEL, n_lo),
                  pl.BlockSpec((D_MODEL, D_MODEL), lambda i: (0, 0)), vec, vec],
        out_specs=[row, row],
        out_shape=[jax.ShapeDtypeStruct((n, D_MODEL), F32), jax.ShapeDtypeStruct((n, D_MODEL), BF16)],
        compiler_params=_cparams(("parallel",)),
        name="outproj",
    )(o_rwkv, o_conv, xp, xs, p["w_out"], p["g_post_mix"], p["g_pre_mlp"])


def _mlp_kernel(n_lo, hm_ref, x1_ref, wup_ref, wdn_ref, g_ref, op_ref, os_ref, acc_ref):
    i = pl.program_id(0)
    j = pl.program_id(1)
    @pl.when(j == 0)
    def _():
        acc_ref[...] = jnp.zeros_like(acc_ref)

    h = _dot(hm_ref[...], wup_ref[...])
    h = jnp.square(jnp.maximum(h, 0.0)).astype(BF16)
    acc_ref[...] += _dot(h, wdn_ref[...])

    def finish(o_ref):
        o_ref[...] = x1_ref[...] + _rms_rows(acc_ref[...], g_ref[...])

    is_last = j == pl.num_programs(1) - 1
    pl.when(is_last & (i < n_lo))(lambda: finish(op_ref))
    pl.when(is_last & (i >= n_lo))(lambda: finish(os_ref))


def _mlp(hm, x1, n_prompt, p, tm=512, tf=1024):
    n = hm.shape[0]
    n_lo = n_prompt // tm
    row = pl.BlockSpec((tm, D_MODEL), lambda i, j: (i, 0))
    return pl.pallas_call(
        functools.partial(_mlp_kernel, n_lo),
        grid=(n // tm, D_FF // tf),
        in_specs=[
            row, row,
            pl.BlockSpec((D_MODEL, tf), lambda i, j: (0, j)),
            pl.BlockSpec((tf, D_MODEL), lambda i, j: (j, 0)),
            pl.BlockSpec((1, D_MODEL), lambda i, j: (0, 0)),
        ],
        out_specs=[_lo_spec(tm, D_MODEL, n_lo), _hi_spec(tm, D_MODEL, n_lo)],
        out_shape=[jax.ShapeDtypeStruct((n_prompt, D_MODEL), F32),
                   jax.ShapeDtypeStruct((n - n_prompt, D_MODEL), F32)],
        scratch_shapes=[pltpu.VMEM((tm, D_MODEL), F32)],
        compiler_params=_cparams(("arbitrary", "arbitrary")),
        name="mlp",
    )(hm, x1, p["w_up"], p["w_down"], p["g_post_mlp"])


def _pad_rows(w, rows, offset):
    out = jnp.zeros((rows, w.shape[1]), w.dtype)
    return out.at[offset:offset + w.shape[0]].set(w)


def _layer_params(l, g_pre_mix, w_in, mu_prev, mu_next, w0_f, w2_f, w0_b, w2_b, a0_f, a2_f, a0_b, a2_b,
                  g2, k_k, k_a, r_k, gn_w, gn_b, dw_w, dw_b, cln_w, cln_b, w_out, g_post_mix,
                  g_pre_mlp, w_up, w_down, g_post_mlp):
    rwkv_cols = 3 * D_RWKV + LORA_COLS
    row = lambda a: a[l].reshape(1, -1).astype(F32)
    pad_cols = lambda a, width: jnp.pad(a, ((0, 0), (0, width - a.shape[1])))
    w = (g_pre_mix[l].astype(F32)[:, None] * w_in[l]).astype(BF16)
    heads = jnp.arange(D_RWKV) // HEAD_DIM
    return dict(
        w_zr=pad_cols(w[:, :rwkv_cols], ZR_COLS), w_zc=w[:, rwkv_cols:],
        mu_prev=pad_cols(row(mu_prev), ZR_COLS), mu_next=pad_cols(row(mu_next), ZR_COLS),
        w2=jnp.concatenate([_pad_rows(w2_f[l], 128, 0), _pad_rows(w2_b[l], 128, DECAY_LORA)], axis=1).astype(BF16),
        a2=jnp.concatenate([_pad_rows(a2_f[l], 128, 0), _pad_rows(a2_b[l], 128, AAA_LORA)], axis=1).astype(BF16),
        g2=_pad_rows(g2[l], 256, 0).astype(BF16),
        w0=jnp.concatenate([row(w0_f), row(w0_b)], axis=1),
        a0=jnp.concatenate([row(a0_f), row(a0_b)], axis=1),
        k_k=row(k_k), k_a=row(k_a), r_k=row(r_k), gn_w=row(gn_w), gn_b=row(gn_b),
        e=(heads[:, None] == heads[None, :]).astype(BF16),
        dw_w=dw_w[l].astype(F32), dw_b=row(dw_b), cln_w=row(cln_w), cln_b=row(cln_b),
        w_out=w_out[l].astype(BF16), g_post_mix=row(g_post_mix), g_pre_mlp=row(g_pre_mlp),
        w_up=w_up[l].astype(BF16), w_down=w_down[l].astype(BF16), g_post_mlp=row(g_post_mlp),
    )


def _layer(xp, xs, seq_len, p):
    n_seq = (xp.shape[0] + xs.shape[0]) // seq_len
    zr = _inproj(xp, xs, p["w_zr"], "inproj_rwkv")
    u_glu = _inproj(xp, xs, p["w_zc"], "inproj_conv", glu=True)
    prep = _prep(zr, seq_len, p)
    g, bonus = prep[9], prep[10]
    yf, yb = _wkv(prep[:9], n_seq, seq_len)
    o_rwkv = _post(yf, yb, g, bonus, p)
    o_conv = _conv(u_glu, seq_len, p)
    x1, hm = _outproj(o_rwkv, o_conv, xp, xs, p)
    return _mlp(hm, x1, xp.shape[0], p)


def _forward(x_prompt, x_sample, weights):
    seq_len = x_prompt.shape[1]
    assert x_sample.shape[1] == seq_len
    xp = x_prompt.reshape(-1, D_MODEL)
    xs = x_sample.reshape(-1, D_MODEL)
    depth = weights[0].shape[0]
    for l in range(depth):
        xp, xs = _layer(xp, xs, seq_len, _layer_params(l, *weights))
    return xp.reshape(x_prompt.shape), xs.reshape(x_sample.shape)


def kernel(x_prompt, x_sample, g_pre_mix, w_in, mu_prev, mu_next, w0_f, w2_f, w0_b, w2_b, a0_f, a2_f, a0_b, a2_b, g2, k_k, k_a, r_k, gn_w, gn_b, dw_w, dw_b, cln_w, cln_b, w_out, g_post_mix, g_pre_mlp, w_up, w_down, g_post_mlp):
    weights = (g_pre_mix, w_in, mu_prev, mu_next, w0_f, w2_f, w0_b, w2_b, a0_f, a2_f, a0_b, a2_b,
               g2, k_k, k_a, r_k, gn_w, gn_b, dw_w, dw_b, cln_w, cln_b, w_out, g_post_mix,
               g_pre_mlp, w_up, w_down, g_post_mlp)
    return _forward(x_prompt, x_sample, weights)
```

```python
import functools
import math

import jax
import jax.numpy as jnp
from jax import lax
from jax.experimental import pallas as pl
from jax.experimental.pallas import tpu as pltpu

F32 = jnp.float32
BF16 = jnp.bfloat16

D_MODEL = 2048
D_RWKV = 1024
D_CONV = 1024
HEAD_DIM = 64
N_HEADS = 16
DECAY_LORA = 64
AAA_LORA = 64
GATE_LORA = 160
CONV_WIDTH = 31
CONV_HALF = CONV_WIDTH // 2
D_FF = 4 * D_MODEL
RMS_EPS = 1e-6
LN_EPS = 1e-5
GN_EPS = 64e-5
L2_EPS = 1e-12

LORA_COLS = 2 * DECAY_LORA + 2 * AAA_LORA + GATE_LORA
LORA_PAD = 512
ZR_COLS = 3 * D_RWKV + LORA_PAD

CHUNK = 64
GROUP = 256
N_GROUPS = D_RWKV // GROUP
WAVE_LAG = 3
HALO = 16
SUBLANES = 8
LANES = 128

VMEM_LIMIT = 56 * 1024 * 1024


def _cparams(sem):
    return pltpu.CompilerParams(dimension_semantics=sem, vmem_limit_bytes=VMEM_LIMIT)


def _dot(a, b):
    return jnp.dot(a, b, preferred_element_type=F32)


def _dot_nt(a, b):
    return lax.dot_general(a, b, (((1,), (1,)), ((), ())), preferred_element_type=F32)


def _dot_tn(a, b):
    return lax.dot_general(a, b, (((0,), (0,)), ((), ())), preferred_element_type=F32)


def _split2(x):
    hi = x.astype(BF16)
    lo = (x - hi.astype(F32)).astype(BF16)
    return hi, lo


def _seg_sum(x, e_ref):
    return _dot(x.astype(BF16), e_ref[...])


def _rms_rows(x, g):
    ms = jnp.mean(x * x, axis=-1, keepdims=True)
    return x * lax.rsqrt(ms + RMS_EPS) * g


def _lo_spec(tm, width, n_lo):
    return pl.BlockSpec((tm, width), lambda i, *_: (jnp.minimum(i, n_lo - 1), 0))


def _hi_spec(tm, width, n_lo):
    return pl.BlockSpec((tm, width), lambda i, *_: (jnp.maximum(i - n_lo, 0), 0))


def _inproj_kernel(n_lo, tn, glu, xp_ref, xs_ref, w_ref, z_ref):
    def run(x_ref):
        x = x_ref[...]
        rinv = lax.rsqrt(jnp.mean(x * x, axis=-1, keepdims=True) + RMS_EPS)
        xb = x.astype(BF16)
        out_cols = z_ref.shape[1]
        for c in range(out_cols // tn):
            cs = slice(c * tn, (c + 1) * tn)
            z = _dot(xb, w_ref[:, cs]) * rinv
            if glu:
                gate = _dot(xb, w_ref[:, out_cols + c * tn:out_cols + (c + 1) * tn]) * rinv
                z = z * jax.nn.sigmoid(gate)
            z_ref[:, cs] = z.astype(z_ref.dtype)

    i = pl.program_id(0)
    pl.when(i < n_lo)(lambda: run(xp_ref))
    pl.when(i >= n_lo)(lambda: run(xs_ref))


def _inproj(xp, xs, w, name, cols=None, glu=False, tm=512, tn=512):
    n_lo = xp.shape[0] // tm
    n = xp.shape[0] + xs.shape[0]
    cols = w.shape[1] if cols is None else cols
    out_cols = cols // 2 if glu else cols
    return pl.pallas_call(
        functools.partial(_inproj_kernel, n_lo, tn, glu),
        grid=(n // tm,),
        in_specs=[
            _lo_spec(tm, D_MODEL, n_lo),
            _hi_spec(tm, D_MODEL, n_lo),
            pl.BlockSpec((D_MODEL, cols), lambda i: (0, 0), pipeline_mode=pl.Buffered(1)),
        ],
        out_specs=pl.BlockSpec((tm, out_cols), lambda i: (i, 0)),
        out_shape=jax.ShapeDtypeStruct((n, out_cols), BF16),
        compiler_params=_cparams(("parallel",)),
        name=name,
    )(xp, xs, w)


def _prep_kernel(seq_len, tb,
                 z_ref, zp_ref, zn_ref, mup_ref, mun_ref, w2_ref, a2_ref, g2_ref,
                 w0_ref, a0_ref, kk_w_ref, ka_ref, rk_ref, e_ref,
                 r_ref, v_ref, kk_ref, lwf_ref, kmf_ref, bf_ref, lwb_ref, kmb_ref, bb_ref,
                 g_ref, bonus_ref):
    i = pl.program_id(0)
    t0 = (i * tb) % seq_len
    first = t0 == 0
    last = (t0 + tb) == seq_len
    edge = lax.broadcasted_iota(jnp.int32, (SUBLANES, 1), 0)

    def shifted(c0, c1):
        z = z_ref[:, c0:c1].astype(F32)
        mp = mup_ref[:, c0:c1]
        mn = mun_ref[:, c0:c1]
        body = (1.0 - mp - mn) * z + mp * pltpu.roll(z, 1, axis=0) + mn * pltpu.roll(z, tb - 1, axis=0)
        prev_row = jnp.where(first, 0.0, zp_ref[HALO - 1:HALO, c0:c1].astype(F32))
        next_row = jnp.where(last, 0.0, zn_ref[0:1, c0:c1].astype(F32))
        top = body[0:SUBLANES] + jnp.where(edge == 0, mp * (prev_row - z[tb - 1:tb]), 0.0)
        bot = body[tb - SUBLANES:tb] + jnp.where(edge == SUBLANES - 1, mn * (next_row - z[0:1]), 0.0)
        return jnp.concatenate([top, body[SUBLANES:tb - SUBLANES], bot], axis=0)

    r = shifted(0, D_RWKV)
    k = shifted(D_RWKV, 2 * D_RWKV)
    v = shifted(2 * D_RWKV, 3 * D_RWKV)
    lora = shifted(3 * D_RWKV, ZR_COLS)

    xw = jnp.tanh(lora[:, 0:128]).astype(BF16)
    xa = lora[:, 128:256].astype(BF16)
    xg = jax.nn.sigmoid(lora[:, 256:512]).astype(BF16)
    dw = _dot(xw, w2_ref[...]) + w0_ref[...]
    da = _dot(xa, a2_ref[...]) + a0_ref[...]
    g_ref[...] = _dot(xg, g2_ref[...]).astype(g_ref.dtype)

    kraw = k * kk_w_ref[...]
    ss = _seg_sum(kraw * kraw, e_ref)
    kk = kraw * lax.rsqrt(jnp.maximum(ss, L2_EPS * L2_EPS))
    rk = _seg_sum(r * k * rk_ref[...], e_ref)
    bonus_ref[...] = (rk * v).astype(bonus_ref.dtype)
    r_ref[...] = r.astype(r_ref.dtype)
    v_ref[...] = v.astype(v_ref.dtype)
    kk_ref[...] = kk.astype(kk_ref.dtype)

    neg_c = -math.exp(-0.5)
    ka = ka_ref[...]
    for d, (lw_ref, km_ref, b_ref) in enumerate(((lwf_ref, kmf_ref, bf_ref), (lwb_ref, kmb_ref, bb_ref))):
        sl = slice(d * D_RWKV, (d + 1) * D_RWKV)
        lw_ref[...] = neg_c * jax.nn.sigmoid(dw[:, sl])
        a = jax.nn.sigmoid(da[:, sl])
        km_ref[...] = (k * (1.0 + (a - 1.0) * ka)).astype(km_ref.dtype)
        b_ref[...] = (kk * a).astype(b_ref.dtype)


_PREP_OUT_DTYPES = (BF16, BF16, BF16, F32, BF16, BF16, F32, BF16, BF16, BF16, BF16)


def _prep(zr, seq_len, p, tb=256):
    n = zr.shape[0]
    hb = tb // HALO
    nhb = n // HALO
    vec = lambda w: pl.BlockSpec((1, w), lambda i: (0, 0))
    full = lambda a: pl.BlockSpec(a.shape, lambda i: (0, 0))
    out_spec = pl.BlockSpec((tb, D_RWKV), lambda i: (i, 0))
    return pl.pallas_call(
        functools.partial(_prep_kernel, seq_len, tb),
        grid=(n // tb,),
        in_specs=[
            pl.BlockSpec((tb, ZR_COLS), lambda i: (i, 0)),
            pl.BlockSpec((HALO, ZR_COLS), lambda i: (jnp.maximum(i * hb - 1, 0), 0)),
            pl.BlockSpec((HALO, ZR_COLS), lambda i: (jnp.minimum((i + 1) * hb, nhb - 1), 0)),
            vec(ZR_COLS), vec(ZR_COLS),
            full(p["w2"]), full(p["a2"]), full(p["g2"]),
            vec(2 * D_RWKV), vec(2 * D_RWKV), vec(D_RWKV), vec(D_RWKV), vec(D_RWKV),
            full(p["e"]),
        ],
        out_specs=[out_spec] * 11,
        out_shape=[jax.ShapeDtypeStruct((n, D_RWKV), dt) for dt in _PREP_OUT_DTYPES],
        compiler_params=_cparams(("parallel",)),
        name="rwkv_prep",
    )(zr, zr, zr, p["mu_prev"], p["mu_next"], p["w2"], p["a2"], p["g2"],
      p["w0"], p["a0"], p["k_k"], p["k_a"], p["r_k"], p["e"])


def _wkv_masks(reverse):
    t = lax.broadcasted_iota(jnp.int32, (CHUNK, GROUP), 0)
    lane = lax.broadcasted_iota(jnp.int32, (CHUNK, GROUP), 1)
    s = lane % CHUNK
    rr = lax.broadcasted_iota(jnp.int32, (LANES, LANES), 0)
    cc = lax.broadcasted_iota(jnp.int32, (LANES, LANES), 1)
    ti = lax.broadcasted_iota(jnp.int32, (CHUNK, CHUNK), 0)
    si = lax.broadcasted_iota(jnp.int32, (CHUNK, CHUNK), 1)
    return dict(
        strict=(s > t) if reverse else (s < t),
        incl=(s >= t) if reverse else (s <= t),
        eye=(s == t).astype(F32),
        bd=(rr // HEAD_DIM) == (cc // HEAD_DIM),
        first_head=lax.broadcasted_iota(jnp.int32, (CHUNK, LANES), 1) < HEAD_DIM,
        tri=((si >= ti) if reverse else (si <= ti)).astype(BF16),
    )


def _bd(x, m):
    zero = jnp.zeros((), x.dtype)
    quad = lambda h: jnp.where(m["bd"], jnp.concatenate([x[:, h * LANES:(h + 1) * LANES]] * 2, axis=0), zero)
    zq = jnp.zeros((LANES, LANES), x.dtype)
    return jnp.concatenate([jnp.concatenate([quad(0), zq], axis=0),
                            jnp.concatenate([zq, quad(1)], axis=0)], axis=1)


def _diag_blocks(full, m):
    halves = []
    for half in range(GROUP // LANES):
        lanes = slice(half * LANES, (half + 1) * LANES)
        r0 = half * LANES
        halves.append(jnp.where(m["first_head"], full[r0:r0 + CHUNK, lanes], full[r0 + CHUNK:r0 + LANES, lanes]))
    return jnp.concatenate(halves, axis=1)


def _wkv_stage0(refs, off, reverse, m):
    r_ref, v_ref, kk_ref, lw_ref, km_ref, b_ref = refs
    sl = pl.ds(off, CHUNK)
    lw = lw_ref[sl, :]
    hi, lo = _split2(lw)
    tri = m["tri"]
    cum = _dot(tri, hi) + _dot(tri, lo)
    tot = cum[0:1, :] if reverse else cum[CHUNK - 1:CHUNK, :]
    e_neg = jnp.exp(-cum)
    e_rest = jnp.exp(tot - cum)
    b = b_ref[sl, :].astype(F32)
    km = km_ref[sl, :].astype(F32)
    return dict(
        a_t=-kk_ref[sl, :].astype(F32) * jnp.exp(cum - lw),
        r_t=r_ref[sl, :].astype(F32) * jnp.exp(cum),
        b_h=b * e_neg, k_h=km * e_neg, b_r=b * e_rest, k_r=km * e_rest,
        v=v_ref[sl, :], e_tot=jnp.exp(tot),
    )


def _wkv_wave(fwd_refs, bwd_refs, off_f, off_b, mf, mb, yf_ref, yb_ref, sf_ref, sb_ref):
    pre_f = _wkv_stage0(fwd_refs, off_f, False, mf)
    pre_b = _wkv_stage0(bwd_refs, off_b, True, mb)
    yield
    insts = []
    for g in range(N_GROUPS):
        insts.append(dict(pre=pre_f, g=g, m=mf, s_ref=sf_ref, y_ref=yf_ref, sl=pl.ds(off_f, CHUNK)))
        insts.append(dict(pre=pre_b, g=g, m=mb, s_ref=sb_ref, y_ref=yb_ref, sl=pl.ds(off_b, CHUNK)))
    rng = range(len(insts))
    ms = [it["m"] for it in insts]
    ls = [slice(it["g"] * GROUP, (it["g"] + 1) * GROUP) for it in insts]
    pre = [it["pre"] for it in insts]
    rt = [pre[i]["r_t"][:, ls[i]] for i in rng]
    at_b = [pre[i]["a_t"][:, ls[i]].astype(BF16) for i in rng]
    v_b = [pre[i]["v"][:, ls[i]].astype(BF16) for i in rng]
    lhs = [jnp.concatenate([at_b[i], rt[i].astype(BF16)], axis=0) for i in rng]
    gb = [_dot_nt(lhs[i], _bd(pre[i]["b_h"][:, ls[i]].astype(BF16), ms[i])) for i in rng]
    yield
    gk = [_dot_nt(lhs[i], _bd(pre[i]["k_h"][:, ls[i]].astype(BF16), ms[i])) for i in rng]
    a_ab = [jnp.where(ms[i]["strict"], gb[i][:CHUNK], 0.0) for i in rng]
    a_rb = [jnp.where(ms[i]["incl"], gb[i][CHUNK:], 0.0).astype(BF16) for i in rng]
    yield
    t_inv = [ms[i]["eye"] + a_ab[i] for i in rng]
    xb = [a.astype(BF16) for a in a_ab]
    x = [_dot(xb[i], _bd(xb[i], ms[i])) for i in rng]
    a_k = [jnp.concatenate([jnp.where(ms[i]["strict"], gk[i][:CHUNK], 0.0),
                            jnp.where(ms[i]["incl"], gk[i][CHUNK:], 0.0)], axis=0).astype(BF16) for i in rng]
    av = [_dot(a_k[i], _bd(v_b[i], ms[i])) for i in rng]
    yield
    for _ in range(int(math.log2(CHUNK)) - 2):
        xb = [xi.astype(BF16) for xi in x]
        res = [_dot(jnp.concatenate([xb[i], t_inv[i].astype(BF16)], axis=0), _bd(xb[i], ms[i])) for i in rng]
        x = [r[:CHUNK] for r in res]
        t_inv = [t_inv[i] + res[i][CHUNK:] for i in rng]
        yield
    res = [_dot(t_inv[i].astype(BF16), _bd(x[i].astype(BF16), ms[i])) for i in rng]
    t_b = [(t_inv[i] + res[i]).astype(BF16) for i in rng]
    bk_b = [jnp.concatenate([pre[i]["b_r"][:, ls[i]], pre[i]["k_r"][:, ls[i]]], axis=0).astype(BF16) for i in rng]
    ard = [jnp.concatenate([at_b[i], rt[i].astype(BF16),
                            (ms[i]["eye"] * pre[i]["e_tot"][:, ls[i]]).astype(BF16)], axis=0) for i in rng]
    yield
    res = [_dot(ard[i], _bd(insts[i]["s_ref"][insts[i]["g"]].astype(BF16), ms[i])) for i in rng]
    yield
    u_b = [_dot(t_b[i], _bd((res[i][:CHUNK] + av[i][:CHUNK]).astype(BF16), ms[i])).astype(BF16) for i in rng]
    yield
    y = [res[i][CHUNK:2 * CHUNK] + av[i][CHUNK:] + _dot(a_rb[i], _bd(u_b[i], ms[i])) for i in rng]
    upd = [_dot_tn(bk_b[i], jnp.concatenate([u_b[i], v_b[i]], axis=0)) for i in rng]
    for i in rng:
        it = insts[i]
        it["y_ref"][it["sl"], ls[i]] = y[i].astype(it["y_ref"].dtype)
        it["s_ref"][it["g"]] = res[i][2 * CHUNK:] + _diag_blocks(upd[i], ms[i])


def _interleave(gens, lag):
    pending = list(gens)
    active = []
    tick = 0
    while pending or active:
        if pending and tick % lag == 0:
            active.append(pending.pop(0))
        for g in list(active):
            if next(g, StopIteration) is StopIteration:
                active.remove(g)
        tick += 1


def _wkv_kernel(n_chunks, *refs):
    fwd_refs = refs[0:6]
    bwd_refs = refs[6:12]
    yf_ref, yb_ref, sf_ref, sb_ref = refs[12:16]

    @pl.when(pl.program_id(1) == 0)
    def _():
        sf_ref[...] = jnp.zeros_like(sf_ref)
        sb_ref[...] = jnp.zeros_like(sb_ref)

    mf = _wkv_masks(False)
    mb = _wkv_masks(True)
    waves = [_wkv_wave(fwd_refs, bwd_refs, c * CHUNK, (n_chunks - 1 - c) * CHUNK, mf, mb,
                       yf_ref, yb_ref, sf_ref, sb_ref) for c in range(n_chunks)]
    _interleave(waves, WAVE_LAG)


def _wkv(pp, n_seq, seq_len, tb=512):
    (r, v, kk, lwf, kmf, bf, lwb, kmb, bb) = pp
    n = r.shape[0]
    nb = seq_len // tb
    fspec = pl.BlockSpec((tb, D_RWKV), lambda s, i: (s * nb + i, 0))
    bspec = pl.BlockSpec((tb, D_RWKV), lambda s, i: (s * nb + nb - 1 - i, 0))
    return pl.pallas_call(
        functools.partial(_wkv_kernel, tb // CHUNK),
        grid=(n_seq, nb),
        in_specs=[fspec] * 6 + [bspec] * 6,
        out_specs=[fspec, bspec],
        out_shape=[jax.ShapeDtypeStruct((n, D_RWKV), BF16)] * 2,
        scratch_shapes=[pltpu.VMEM((N_GROUPS, CHUNK, GROUP), F32)] * 2,
        compiler_params=_cparams(("parallel", "arbitrary")),
        name="wkv",
    )(r, v, kk, lwf, kmf, bf, r, v, kk, lwb, kmb, bb)


def _post_kernel(yf_ref, yb_ref, g_ref, bonus_ref, gnw_ref, gnb_ref, e_ref, o_ref):
    y = yf_ref[...].astype(F32) + yb_ref[...].astype(F32)
    mu = _seg_sum(y, e_ref) * (1.0 / HEAD_DIM)
    d = y - mu
    var = _seg_sum(d * d, e_ref) * (1.0 / HEAD_DIM)
    yn = d * lax.rsqrt(var + GN_EPS) * gnw_ref[...] + gnb_ref[...]
    o_ref[...] = ((yn + bonus_ref[...].astype(F32)) * g_ref[...].astype(F32)).astype(o_ref.dtype)


def _post(yf, yb, g, bonus, p, tb=512):
    n = yf.shape[0]
    spec = pl.BlockSpec((tb, D_RWKV), lambda i: (i, 0))
    vec = pl.BlockSpec((1, D_RWKV), lambda i: (0, 0))
    return pl.pallas_call(
        _post_kernel,
        grid=(n // tb,),
        in_specs=[spec] * 4 + [vec, vec, pl.BlockSpec(p["e"].shape, lambda i: (0, 0))],
        out_specs=spec,
        out_shape=jax.ShapeDtypeStruct((n, D_RWKV), BF16),
        compiler_params=_cparams(("parallel",)),
        name="rwkv_post",
    )(yf, yb, g, bonus, p["gn_w"], p["gn_b"], p["e"])


CONV_ROWS = 64


def _conv_kernel(seq_len, tb, u_ref, up_ref, un_ref, dww_ref, dwb_ref, lnw_ref, lnb_ref,
                 o_ref, u_scr, sh_scr, c_scr):
    i = pl.program_id(0)
    t0 = (i * tb) % seq_len
    first = t0 == 0
    last = (t0 + tb) == seq_len

    u_scr[HALO:HALO + tb, :] = u_ref[...].astype(F32)
    u_scr[0:HALO, :] = jnp.where(first, 0.0, up_ref[...].astype(F32))
    u_scr[HALO + tb:2 * HALO + tb, :] = jnp.where(last, 0.0, un_ref[...].astype(F32))

    sh_rows = tb + 2 * HALO - SUBLANES
    for s in range(1, SUBLANES):
        sh_scr[s - 1] = u_scr[s:s + sh_rows, :]

    base = HALO - CONV_HALF
    for cb in range(D_CONV // LANES):
        cs = slice(cb * LANES, (cb + 1) * LANES)
        wcol = dww_ref[:, cs]
        bias = dwb_ref[:, cs]
        for rc in range(tb // CONV_ROWS):
            r0 = rc * CONV_ROWS
            acc = jnp.zeros((CONV_ROWS, LANES), F32) + bias
            for j in range(CONV_WIDTH):
                s = (base + j) % SUBLANES
                q = r0 + base + j - s
                if s == 0:
                    u = u_scr[q:q + CONV_ROWS, cs]
                else:
                    u = sh_scr[s - 1, q:q + CONV_ROWS, cs]
                acc = acc + u * wcol[j:j + 1, :]
            c_scr[r0:r0 + CONV_ROWS, cs] = acc

    c = c_scr[...]
    cm = jnp.mean(c, axis=-1, keepdims=True)
    d = c - cm
    cv = jnp.mean(d * d, axis=-1, keepdims=True)
    y = d * lax.rsqrt(cv + LN_EPS) * lnw_ref[...] + lnb_ref[...]
    o_ref[...] = (y * jax.nn.sigmoid(y)).astype(o_ref.dtype)


def _conv(u, seq_len, p, tb=256):
    n = u.shape[0]
    hb = tb // HALO
    nhb = n // HALO
    vec = pl.BlockSpec((1, D_CONV), lambda i: (0, 0))
    return pl.pallas_call(
        functools.partial(_conv_kernel, seq_len, tb),
        grid=(n // tb,),
        in_specs=[
            pl.BlockSpec((tb, D_CONV), lambda i: (i, 0)),
            pl.BlockSpec((HALO, D_CONV), lambda i: (jnp.maximum(i * hb - 1, 0), 0)),
            pl.BlockSpec((HALO, D_CONV), lambda i: (jnp.minimum((i + 1) * hb, nhb - 1), 0)),
            pl.BlockSpec((CONV_WIDTH, D_CONV), lambda i: (0, 0)),
            vec, vec, vec,
        ],
        out_specs=pl.BlockSpec((tb, D_CONV), lambda i: (i, 0)),
        out_shape=jax.ShapeDtypeStruct((n, D_CONV), BF16),
        scratch_shapes=[
            pltpu.VMEM((tb + 2 * HALO, D_CONV), F32),
            pltpu.VMEM((SUBLANES - 1, tb + 2 * HALO - SUBLANES, D_CONV), F32),
            pltpu.VMEM((tb, D_CONV), F32),
        ],
        compiler_params=_cparams(("parallel",)),
        name="conv_module",
    )(u, u, u, p["dw_w"], p["dw_b"], p["cln_w"], p["cln_b"])


OUTPROJ_ROWS = 256


def _outproj_kernel(n_lo, orw_ref, ocv_ref, xp_ref, xs_ref, w_ref, gpost_ref, gpre_ref, x1_ref, hm_ref):
    from_prompt = pl.program_id(0) < n_lo
    tm = x1_ref.shape[0]
    for r0 in range(0, tm, OUTPROJ_ROWS):
        rs = slice(r0, r0 + OUTPROJ_ROWS)
        mix = _dot(orw_ref[rs, :], w_ref[0:D_RWKV, :]) + _dot(ocv_ref[rs, :], w_ref[D_RWKV:D_MODEL, :])
        x = jnp.where(from_prompt, xp_ref[rs, :], xs_ref[rs, :])
        x1 = x + _rms_rows(mix, gpost_ref[...])
        x1_ref[rs, :] = x1
        hm_ref[rs, :] = _rms_rows(x1, gpre_ref[...]).astype(hm_ref.dtype)


def _outproj(o_rwkv, o_conv, xp, xs, p, tm=512):
    n = o_rwkv.shape[0]
    n_lo = xp.shape[0] // tm
    half = pl.BlockSpec((tm, D_RWKV), lambda i: (i, 0))
    row = pl.BlockSpec((tm, D_MODEL), lambda i: (i, 0))
    vec = pl.BlockSpec((1, D_MODEL), lambda i: (0, 0))
    return pl.pallas_call(
        functools.partial(_outproj_kernel, n_lo),
        grid=(n // tm,),
        in_specs=[half, half, _lo_spec(tm, D_MODEL, n_lo), _hi_spec(tm, D_MODEL, n_lo),
                  pl.BlockSpec((D_MODEL, D_MODEL), lambda i: (0, 0)), vec, vec],
        out_specs=[row, row],
        out_shape=[jax.ShapeDtypeStruct((n, D_MODEL), F32), jax.ShapeDtypeStruct((n, D_MODEL), BF16)],
        compiler_params=_cparams(("parallel",)),
        name="outproj",
    )(o_rwkv, o_conv, xp, xs, p["w_out"], p["g_post_mix"], p["g_pre_mlp"])


def _mlp_kernel(n_lo, hm_ref, x1_ref, wup_ref, wdn_ref, g_ref, op_ref, os_ref, acc_ref):
    i = pl.program_id(0)
    j = pl.program_id(1)
    @pl.when(j == 0)
    def _():
        acc_ref[...] = jnp.zeros_like(acc_ref)

    h = _dot(hm_ref[...], wup_ref[...])
    h = jnp.square(jnp.maximum(h, 0.0)).astype(BF16)
    acc_ref[...] += _dot(h, wdn_ref[...])

    def finish(o_ref):
        o_ref[...] = x1_ref[...] + _rms_rows(acc_ref[...], g_ref[...])

    is_last = j == pl.num_programs(1) - 1
    pl.when(is_last & (i < n_lo))(lambda: finish(op_ref))
    pl.when(is_last & (i >= n_lo))(lambda: finish(os_ref))


def _mlp(hm, x1, n_prompt, p, tm=512, tf=1024):
    n = hm.shape[0]
    n_lo = n_prompt // tm
    row = pl.BlockSpec((tm, D_MODEL), lambda i, j: (i, 0))
    return pl.pallas_call(
        functools.partial(_mlp_kernel, n_lo),
        grid=(n // tm, D_FF // tf),
        in_specs=[
            row, row,
            pl.BlockSpec((D_MODEL, tf), lambda i, j: (0, j)),
            pl.BlockSpec((tf, D_MODEL), lambda i, j: (j, 0)),
            pl.BlockSpec((1, D_MODEL), lambda i, j: (0, 0)),
        ],
        out_specs=[_lo_spec(tm, D_MODEL, n_lo), _hi_spec(tm, D_MODEL, n_lo)],
        out_shape=[jax.ShapeDtypeStruct((n_prompt, D_MODEL), F32),
                   jax.ShapeDtypeStruct((n - n_prompt, D_MODEL), F32)],
        scratch_shapes=[pltpu.VMEM((tm, D_MODEL), F32)],
        compiler_params=_cparams(("arbitrary", "arbitrary")),
        name="mlp",
    )(hm, x1, p["w_up"], p["w_down"], p["g_post_mlp"])


def _pad_rows(w, rows, offset):
    out = jnp.zeros((rows, w.shape[1]), w.dtype)
    return out.at[offset:offset + w.shape[0]].set(w)


def _layer_params(l, g_pre_mix, w_in, mu_prev, mu_next, w0_f, w2_f, w0_b, w2_b, a0_f, a2_f, a0_b, a2_b,
                  g2, k_k, k_a, r_k, gn_w, gn_b, dw_w, dw_b, cln_w, cln_b, w_out, g_post_mix,
                  g_pre_mlp, w_up, w_down, g_post_mlp):
    rwkv_cols = 3 * D_RWKV + LORA_COLS
    row = lambda a: a[l].reshape(1, -1).astype(F32)
    pad_cols = lambda a, width: jnp.pad(a, ((0, 0), (0, width - a.shape[1])))
    w = (g_pre_mix[l].astype(F32)[:, None] * w_in[l]).astype(BF16)
    heads = jnp.arange(D_RWKV) // HEAD_DIM
    return dict(
        w_in=w, w_zc=w[:, rwkv_cols:],
        mu_prev=pad_cols(row(mu_prev), ZR_COLS), mu_next=pad_cols(row(mu_next), ZR_COLS),
        w2=jnp.concatenate([_pad_rows(w2_f[l], 128, 0), _pad_rows(w2_b[l], 128, DECAY_LORA)], axis=1).astype(BF16),
        a2=jnp.concatenate([_pad_rows(a2_f[l], 128, 0), _pad_rows(a2_b[l], 128, AAA_LORA)], axis=1).astype(BF16),
        g2=_pad_rows(g2[l], 256, 0).astype(BF16),
        w0=jnp.concatenate([row(w0_f), row(w0_b)], axis=1),
        a0=jnp.concatenate([row(a0_f), row(a0_b)], axis=1),
        k_k=row(k_k), k_a=row(k_a), r_k=row(r_k), gn_w=row(gn_w), gn_b=row(gn_b),
        e=(heads[:, None] == heads[None, :]).astype(BF16),
        dw_w=dw_w[l].astype(F32), dw_b=row(dw_b), cln_w=row(cln_w), cln_b=row(cln_b),
        w_out=w_out[l].astype(BF16), g_post_mix=row(g_post_mix), g_pre_mlp=row(g_pre_mlp),
        w_up=w_up[l].astype(BF16), w_down=w_down[l].astype(BF16), g_post_mlp=row(g_post_mlp),
    )


def _layer(xp, xs, seq_len, p):
    n_seq = (xp.shape[0] + xs.shape[0]) // seq_len
    zr = _inproj(xp, xs, p["w_in"], "inproj_rwkv", cols=ZR_COLS)
    u_glu = _inproj(xp, xs, p["w_zc"], "inproj_conv", glu=True)
    prep = _prep(zr, seq_len, p)
    g, bonus = prep[9], prep[10]
    yf, yb = _wkv(prep[:9], n_seq, seq_len)
    o_rwkv = _post(yf, yb, g, bonus, p)
    o_conv = _conv(u_glu, seq_len, p)
    x1, hm = _outproj(o_rwkv, o_conv, xp, xs, p)
    return _mlp(hm, x1, xp.shape[0], p)


def _forward(x_prompt, x_sample, weights):
    seq_len = x_prompt.shape[1]
    assert x_sample.shape[1] == seq_len
    xp = x_prompt.reshape(-1, D_MODEL)
    xs = x_sample.reshape(-1, D_MODEL)
    depth = weights[0].shape[0]
    for l in range(depth):
        xp, xs = _layer(xp, xs, seq_len, _layer_params(l, *weights))
    return xp.reshape(x_prompt.shape), xs.reshape(x_sample.shape)


def kernel(x_prompt, x_sample, g_pre_mix, w_in, mu_prev, mu_next, w0_f, w2_f, w0_b, w2_b, a0_f, a2_f, a0_b, a2_b, g2, k_k, k_a, r_k, gn_w, gn_b, dw_w, dw_b, cln_w, cln_b, w_out, g_post_mix, g_pre_mlp, w_up, w_down, g_post_mlp):
    weights = (g_pre_mix, w_in, mu_prev, mu_next, w0_f, w2_f, w0_b, w2_b, a0_f, a2_f, a0_b, a2_b,
               g2, k_k, k_a, r_k, gn_w, gn_b, dw_w, dw_b, cln_w, cln_b, w_out, g_post_mix,
               g_pre_mlp, w_up, w_down, g_post_mlp)
    return _forward(x_prompt, x_sample, weights)
```

```python
import functools
import math

import jax
import jax.numpy as jnp
from jax import lax
from jax.experimental import pallas as pl
from jax.experimental.pallas import tpu as pltpu

F32 = jnp.float32
BF16 = jnp.bfloat16

D_MODEL = 2048
D_RWKV = 1024
D_CONV = 1024
HEAD_DIM = 64
N_HEADS = 16
DECAY_LORA = 64
AAA_LORA = 64
GATE_LORA = 160
CONV_WIDTH = 31
CONV_HALF = CONV_WIDTH // 2
D_FF = 4 * D_MODEL
RMS_EPS = 1e-6
LN_EPS = 1e-5
GN_EPS = 64e-5
L2_EPS = 1e-12

LORA_COLS = 2 * DECAY_LORA + 2 * AAA_LORA + GATE_LORA
LORA_PAD = 512
ZR_COLS = 3 * D_RWKV + LORA_PAD

CHUNK = 64
GROUP = 256
N_GROUPS = D_RWKV // GROUP
WAVE_LAG = 3
HALO = 16
SUBLANES = 8
LANES = 128

VMEM_LIMIT = 56 * 1024 * 1024


def _cparams(sem):
    return pltpu.CompilerParams(dimension_semantics=sem, vmem_limit_bytes=VMEM_LIMIT)


def _dot(a, b):
    return jnp.dot(a, b, preferred_element_type=F32)


def _dot_nt(a, b):
    return lax.dot_general(a, b, (((1,), (1,)), ((), ())), preferred_element_type=F32)


def _dot_tn(a, b):
    return lax.dot_general(a, b, (((0,), (0,)), ((), ())), preferred_element_type=F32)


def _split2(x):
    hi = x.astype(BF16)
    lo = (x - hi.astype(F32)).astype(BF16)
    return hi, lo


def _seg_sum(x, e_ref):
    return _dot(x.astype(BF16), e_ref[...])


def _rms_rows(x, g):
    ms = jnp.mean(x * x, axis=-1, keepdims=True)
    return x * lax.rsqrt(ms + RMS_EPS) * g


def _lo_spec(tm, width, n_lo):
    return pl.BlockSpec((tm, width), lambda i, *_: (jnp.minimum(i, n_lo - 1), 0))


def _hi_spec(tm, width, n_lo):
    return pl.BlockSpec((tm, width), lambda i, *_: (jnp.maximum(i - n_lo, 0), 0))


def _inproj_kernel(n_lo, tn, glu, xp_ref, xs_ref, w_ref, z_ref):
    def run(x_ref):
        x = x_ref[...]
        rinv = lax.rsqrt(jnp.mean(x * x, axis=-1, keepdims=True) + RMS_EPS)
        xb = x.astype(BF16)
        out_cols = z_ref.shape[1]
        for c in range(out_cols // tn):
            cs = slice(c * tn, (c + 1) * tn)
            z = _dot(xb, w_ref[:, cs]) * rinv
            if glu:
                gate = _dot(xb, w_ref[:, out_cols + c * tn:out_cols + (c + 1) * tn]) * rinv
                z = z * jax.nn.sigmoid(gate)
            z_ref[:, cs] = z.astype(z_ref.dtype)

    i = pl.program_id(0)
    pl.when(i < n_lo)(lambda: run(xp_ref))
    pl.when(i >= n_lo)(lambda: run(xs_ref))


def _inproj(xp, xs, w, name, cols=None, glu=False, tm=512, tn=512):
    n_lo = xp.shape[0] // tm
    n = xp.shape[0] + xs.shape[0]
    cols = w.shape[1] if cols is None else cols
    out_cols = cols // 2 if glu else cols
    return pl.pallas_call(
        functools.partial(_inproj_kernel, n_lo, tn, glu),
        grid=(n // tm,),
        in_specs=[
            _lo_spec(tm, D_MODEL, n_lo),
            _hi_spec(tm, D_MODEL, n_lo),
            pl.BlockSpec((D_MODEL, cols), lambda i: (0, 0), pipeline_mode=pl.Buffered(1)),
        ],
        out_specs=pl.BlockSpec((tm, out_cols), lambda i: (i, 0)),
        out_shape=jax.ShapeDtypeStruct((n, out_cols), BF16),
        compiler_params=_cparams(("parallel",)),
        name=name,
    )(xp, xs, w)


def _prep_kernel(seq_len, tb,
                 z_ref, zp_ref, zn_ref, mup_ref, mun_ref, w2_ref, a2_ref, g2_ref,
                 w0_ref, a0_ref, kk_w_ref, ka_ref, rk_ref, e_ref,
                 r_ref, v_ref, kk_ref, lwf_ref, kmf_ref, bf_ref, lwb_ref, kmb_ref, bb_ref,
                 g_ref, bonus_ref):
    i = pl.program_id(0)
    t0 = (i * tb) % seq_len
    first = t0 == 0
    last = (t0 + tb) == seq_len
    edge = lax.broadcasted_iota(jnp.int32, (SUBLANES, 1), 0)

    def shifted(c0, c1):
        z = z_ref[:, c0:c1].astype(F32)
        mp = mup_ref[:, c0:c1]
        mn = mun_ref[:, c0:c1]
        body = (1.0 - mp - mn) * z + mp * pltpu.roll(z, 1, axis=0) + mn * pltpu.roll(z, tb - 1, axis=0)
        prev_row = jnp.where(first, 0.0, zp_ref[HALO - 1:HALO, c0:c1].astype(F32))
        next_row = jnp.where(last, 0.0, zn_ref[0:1, c0:c1].astype(F32))
        top = body[0:SUBLANES] + jnp.where(edge == 0, mp * (prev_row - z[tb - 1:tb]), 0.0)
        bot = body[tb - SUBLANES:tb] + jnp.where(edge == SUBLANES - 1, mn * (next_row - z[0:1]), 0.0)
        return jnp.concatenate([top, body[SUBLANES:tb - SUBLANES], bot], axis=0)

    r = shifted(0, D_RWKV)
    k = shifted(D_RWKV, 2 * D_RWKV)
    v = shifted(2 * D_RWKV, 3 * D_RWKV)
    lora = shifted(3 * D_RWKV, ZR_COLS)

    xw = jnp.tanh(lora[:, 0:128]).astype(BF16)
    xa = lora[:, 128:256].astype(BF16)
    xg = jax.nn.sigmoid(lora[:, 256:512]).astype(BF16)
    dw = _dot(xw, w2_ref[...]) + w0_ref[...]
    da = _dot(xa, a2_ref[...]) + a0_ref[...]
    g_ref[...] = _dot(xg, g2_ref[...]).astype(g_ref.dtype)

    kraw = k * kk_w_ref[...]
    ss = _seg_sum(kraw * kraw, e_ref)
    kk = kraw * lax.rsqrt(jnp.maximum(ss, L2_EPS * L2_EPS))
    rk = _seg_sum(r * k * rk_ref[...], e_ref)
    bonus_ref[...] = (rk * v).astype(bonus_ref.dtype)
    r_ref[...] = r.astype(r_ref.dtype)
    v_ref[...] = v.astype(v_ref.dtype)
    kk_ref[...] = kk.astype(kk_ref.dtype)

    neg_c = -math.exp(-0.5)
    ka = ka_ref[...]
    for d, (lw_ref, km_ref, b_ref) in enumerate(((lwf_ref, kmf_ref, bf_ref), (lwb_ref, kmb_ref, bb_ref))):
        sl = slice(d * D_RWKV, (d + 1) * D_RWKV)
        lw_ref[...] = neg_c * jax.nn.sigmoid(dw[:, sl])
        a = jax.nn.sigmoid(da[:, sl])
        km_ref[...] = (k * (1.0 + (a - 1.0) * ka)).astype(km_ref.dtype)
        b_ref[...] = (kk * a).astype(b_ref.dtype)


_PREP_OUT_DTYPES = (BF16, BF16, BF16, F32, BF16, BF16, F32, BF16, BF16, BF16, BF16)


def _prep(zr, seq_len, p, tb=512):
    n = zr.shape[0]
    hb = tb // HALO
    nhb = n // HALO
    vec = lambda w: pl.BlockSpec((1, w), lambda i: (0, 0))
    full = lambda a: pl.BlockSpec(a.shape, lambda i: (0, 0))
    out_spec = pl.BlockSpec((tb, D_RWKV), lambda i: (i, 0))
    return pl.pallas_call(
        functools.partial(_prep_kernel, seq_len, tb),
        grid=(n // tb,),
        in_specs=[
            pl.BlockSpec((tb, ZR_COLS), lambda i: (i, 0)),
            pl.BlockSpec((HALO, ZR_COLS), lambda i: (jnp.maximum(i * hb - 1, 0), 0)),
            pl.BlockSpec((HALO, ZR_COLS), lambda i: (jnp.minimum((i + 1) * hb, nhb - 1), 0)),
            vec(ZR_COLS), vec(ZR_COLS),
            full(p["w2"]), full(p["a2"]), full(p["g2"]),
            vec(2 * D_RWKV), vec(2 * D_RWKV), vec(D_RWKV), vec(D_RWKV), vec(D_RWKV),
            full(p["e"]),
        ],
        out_specs=[out_spec] * 11,
        out_shape=[jax.ShapeDtypeStruct((n, D_RWKV), dt) for dt in _PREP_OUT_DTYPES],
        compiler_params=_cparams(("parallel",)),
        name="rwkv_prep",
    )(zr, zr, zr, p["mu_prev"], p["mu_next"], p["w2"], p["a2"], p["g2"],
      p["w0"], p["a0"], p["k_k"], p["k_a"], p["r_k"], p["e"])


def _wkv_masks(reverse):
    t = lax.broadcasted_iota(jnp.int32, (CHUNK, GROUP), 0)
    lane = lax.broadcasted_iota(jnp.int32, (CHUNK, GROUP), 1)
    s = lane % CHUNK
    rr = lax.broadcasted_iota(jnp.int32, (LANES, LANES), 0)
    cc = lax.broadcasted_iota(jnp.int32, (LANES, LANES), 1)
    ti = lax.broadcasted_iota(jnp.int32, (CHUNK, CHUNK), 0)
    si = lax.broadcasted_iota(jnp.int32, (CHUNK, CHUNK), 1)
    return dict(
        strict=(s > t) if reverse else (s < t),
        incl=(s >= t) if reverse else (s <= t),
        eye=(s == t).astype(F32),
        bd=(rr // HEAD_DIM) == (cc // HEAD_DIM),
        first_head=lax.broadcasted_iota(jnp.int32, (CHUNK, LANES), 1) < HEAD_DIM,
        tri=((si >= ti) if reverse else (si <= ti)).astype(BF16),
    )


def _bd(x, m):
    zero = jnp.zeros((), x.dtype)
    quad = lambda h: jnp.where(m["bd"], jnp.concatenate([x[:, h * LANES:(h + 1) * LANES]] * 2, axis=0), zero)
    zq = jnp.zeros((LANES, LANES), x.dtype)
    return jnp.concatenate([jnp.concatenate([quad(0), zq], axis=0),
                            jnp.concatenate([zq, quad(1)], axis=0)], axis=1)


def _diag_blocks(full, m):
    halves = []
    for half in range(GROUP // LANES):
        lanes = slice(half * LANES, (half + 1) * LANES)
        r0 = half * LANES
        halves.append(jnp.where(m["first_head"], full[r0:r0 + CHUNK, lanes], full[r0 + CHUNK:r0 + LANES, lanes]))
    return jnp.concatenate(halves, axis=1)


def _wkv_stage0(refs, off, reverse, m):
    r_ref, v_ref, kk_ref, lw_ref, km_ref, b_ref = refs
    sl = pl.ds(off, CHUNK)
    lw = lw_ref[sl, :]
    hi, lo = _split2(lw)
    tri = m["tri"]
    cum = _dot(tri, hi) + _dot(tri, lo)
    tot = cum[0:1, :] if reverse else cum[CHUNK - 1:CHUNK, :]
    e_neg = jnp.exp(-cum)
    e_rest = jnp.exp(tot - cum)
    b = b_ref[sl, :].astype(F32)
    km = km_ref[sl, :].astype(F32)
    return dict(
        a_t=-kk_ref[sl, :].astype(F32) * jnp.exp(cum - lw),
        r_t=r_ref[sl, :].astype(F32) * jnp.exp(cum),
        b_h=b * e_neg, k_h=km * e_neg, b_r=b * e_rest, k_r=km * e_rest,
        v=v_ref[sl, :], e_tot=jnp.exp(tot),
    )


def _wkv_wave(fwd_refs, bwd_refs, off_f, off_b, mf, mb, yf_ref, yb_ref, sf_ref, sb_ref):
    pre_f = _wkv_stage0(fwd_refs, off_f, False, mf)
    pre_b = _wkv_stage0(bwd_refs, off_b, True, mb)
    yield
    insts = []
    for g in range(N_GROUPS):
        insts.append(dict(pre=pre_f, g=g, m=mf, s_ref=sf_ref, y_ref=yf_ref, sl=pl.ds(off_f, CHUNK)))
        insts.append(dict(pre=pre_b, g=g, m=mb, s_ref=sb_ref, y_ref=yb_ref, sl=pl.ds(off_b, CHUNK)))
    rng = range(len(insts))
    ms = [it["m"] for it in insts]
    ls = [slice(it["g"] * GROUP, (it["g"] + 1) * GROUP) for it in insts]
    pre = [it["pre"] for it in insts]
    rt = [pre[i]["r_t"][:, ls[i]] for i in rng]
    at_b = [pre[i]["a_t"][:, ls[i]].astype(BF16) for i in rng]
    v_b = [pre[i]["v"][:, ls[i]].astype(BF16) for i in rng]
    lhs = [jnp.concatenate([at_b[i], rt[i].astype(BF16)], axis=0) for i in rng]
    gb = [_dot_nt(lhs[i], _bd(pre[i]["b_h"][:, ls[i]].astype(BF16), ms[i])) for i in rng]
    yield
    gk = [_dot_nt(lhs[i], _bd(pre[i]["k_h"][:, ls[i]].astype(BF16), ms[i])) for i in rng]
    a_ab = [jnp.where(ms[i]["strict"], gb[i][:CHUNK], 0.0) for i in rng]
    a_rb = [jnp.where(ms[i]["incl"], gb[i][CHUNK:], 0.0).astype(BF16) for i in rng]
    yield
    t_inv = [ms[i]["eye"] + a_ab[i] for i in rng]
    xb = [a.astype(BF16) for a in a_ab]
    x = [_dot(xb[i], _bd(xb[i], ms[i])) for i in rng]
    a_k = [jnp.concatenate([jnp.where(ms[i]["strict"], gk[i][:CHUNK], 0.0),
                            jnp.where(ms[i]["incl"], gk[i][CHUNK:], 0.0)], axis=0).astype(BF16) for i in rng]
    av = [_dot(a_k[i], _bd(v_b[i], ms[i])) for i in rng]
    yield
    for _ in range(int(math.log2(CHUNK)) - 2):
        xb = [xi.astype(BF16) for xi in x]
        res = [_dot(jnp.concatenate([xb[i], t_inv[i].astype(BF16)], axis=0), _bd(xb[i], ms[i])) for i in rng]
        x = [r[:CHUNK] for r in res]
        t_inv = [t_inv[i] + res[i][CHUNK:] for i in rng]
        yield
    res = [_dot(t_inv[i].astype(BF16), _bd(x[i].astype(BF16), ms[i])) for i in rng]
    t_b = [(t_inv[i] + res[i]).astype(BF16) for i in rng]
    bk_b = [jnp.concatenate([pre[i]["b_r"][:, ls[i]], pre[i]["k_r"][:, ls[i]]], axis=0).astype(BF16) for i in rng]
    ard = [jnp.concatenate([at_b[i], rt[i].astype(BF16),
                            (ms[i]["eye"] * pre[i]["e_tot"][:, ls[i]]).astype(BF16)], axis=0) for i in rng]
    yield
    res = [_dot(ard[i], _bd(insts[i]["s_ref"][insts[i]["g"]].astype(BF16), ms[i])) for i in rng]
    yield
    u_b = [_dot(t_b[i], _bd((res[i][:CHUNK] + av[i][:CHUNK]).astype(BF16), ms[i])).astype(BF16) for i in rng]
    yield
    y = [res[i][CHUNK:2 * CHUNK] + av[i][CHUNK:] + _dot(a_rb[i], _bd(u_b[i], ms[i])) for i in rng]
    upd = [_dot_tn(bk_b[i], jnp.concatenate([u_b[i], v_b[i]], axis=0)) for i in rng]
    for i in rng:
        it = insts[i]
        it["y_ref"][it["sl"], ls[i]] = y[i].astype(it["y_ref"].dtype)
        it["s_ref"][it["g"]] = res[i][2 * CHUNK:] + _diag_blocks(upd[i], ms[i])


def _interleave(gens, lag):
    pending = list(gens)
    active = []
    tick = 0
    while pending or active:
        if pending and tick % lag == 0:
            active.append(pending.pop(0))
        for g in list(active):
            if next(g, StopIteration) is StopIteration:
                active.remove(g)
        tick += 1


def _wkv_kernel(n_chunks, *refs):
    fwd_refs = refs[0:6]
    bwd_refs = refs[6:12]
    yf_ref, yb_ref, sf_ref, sb_ref = refs[12:16]

    @pl.when(pl.program_id(1) == 0)
    def _():
        sf_ref[...] = jnp.zeros_like(sf_ref)
        sb_ref[...] = jnp.zeros_like(sb_ref)

    mf = _wkv_masks(False)
    mb = _wkv_masks(True)
    waves = [_wkv_wave(fwd_refs, bwd_refs, c * CHUNK, (n_chunks - 1 - c) * CHUNK, mf, mb,
                       yf_ref, yb_ref, sf_ref, sb_ref) for c in range(n_chunks)]
    _interleave(waves, WAVE_LAG)


def _wkv(pp, n_seq, seq_len, tb=512):
    (r, v, kk, lwf, kmf, bf, lwb, kmb, bb) = pp
    n = r.shape[0]
    nb = seq_len // tb
    fspec = pl.BlockSpec((tb, D_RWKV), lambda s, i: (s * nb + i, 0))
    bspec = pl.BlockSpec((tb, D_RWKV), lambda s, i: (s * nb + nb - 1 - i, 0))
    return pl.pallas_call(
        functools.partial(_wkv_kernel, tb // CHUNK),
        grid=(n_seq, nb),
        in_specs=[fspec] * 6 + [bspec] * 6,
        out_specs=[fspec, bspec],
        out_shape=[jax.ShapeDtypeStruct((n, D_RWKV), BF16)] * 2,
        scratch_shapes=[pltpu.VMEM((N_GROUPS, CHUNK, GROUP), F32)] * 2,
        compiler_params=_cparams(("parallel", "arbitrary")),
        name="wkv",
    )(r, v, kk, lwf, kmf, bf, r, v, kk, lwb, kmb, bb)


def _post_kernel(yf_ref, yb_ref, g_ref, bonus_ref, gnw_ref, gnb_ref, e_ref, o_ref):
    y = yf_ref[...].astype(F32) + yb_ref[...].astype(F32)
    mu = _seg_sum(y, e_ref) * (1.0 / HEAD_DIM)
    d = y - mu
    var = _seg_sum(d * d, e_ref) * (1.0 / HEAD_DIM)
    yn = d * lax.rsqrt(var + GN_EPS) * gnw_ref[...] + gnb_ref[...]
    o_ref[...] = ((yn + bonus_ref[...].astype(F32)) * g_ref[...].astype(F32)).astype(o_ref.dtype)


def _post(yf, yb, g, bonus, p, tb=512):
    n = yf.shape[0]
    spec = pl.BlockSpec((tb, D_RWKV), lambda i: (i, 0))
    vec = pl.BlockSpec((1, D_RWKV), lambda i: (0, 0))
    return pl.pallas_call(
        _post_kernel,
        grid=(n // tb,),
        in_specs=[spec] * 4 + [vec, vec, pl.BlockSpec(p["e"].shape, lambda i: (0, 0))],
        out_specs=spec,
        out_shape=jax.ShapeDtypeStruct((n, D_RWKV), BF16),
        compiler_params=_cparams(("parallel",)),
        name="rwkv_post",
    )(yf, yb, g, bonus, p["gn_w"], p["gn_b"], p["e"])


CONV_ROWS = 64


def _conv_kernel(seq_len, tb, u_ref, up_ref, un_ref, dww_ref, dwb_ref, lnw_ref, lnb_ref,
                 o_ref, u_scr, sh_scr, c_scr):
    i = pl.program_id(0)
    t0 = (i * tb) % seq_len
    first = t0 == 0
    last = (t0 + tb) == seq_len

    u_scr[HALO:HALO + tb, :] = u_ref[...].astype(F32)
    u_scr[0:HALO, :] = jnp.where(first, 0.0, up_ref[...].astype(F32))
    u_scr[HALO + tb:2 * HALO + tb, :] = jnp.where(last, 0.0, un_ref[...].astype(F32))

    sh_rows = tb + 2 * HALO - SUBLANES
    for s in range(1, SUBLANES):
        sh_scr[s - 1] = u_scr[s:s + sh_rows, :]

    base = HALO - CONV_HALF
    for cb in range(D_CONV // LANES):
        cs = slice(cb * LANES, (cb + 1) * LANES)
        wcol = dww_ref[:, cs]
        bias = dwb_ref[:, cs]
        for rc in range(tb // CONV_ROWS):
            r0 = rc * CONV_ROWS
            acc = jnp.zeros((CONV_ROWS, LANES), F32) + bias
            for j in range(CONV_WIDTH):
                s = (base + j) % SUBLANES
                q = r0 + base + j - s
                if s == 0:
                    u = u_scr[q:q + CONV_ROWS, cs]
                else:
                    u = sh_scr[s - 1, q:q + CONV_ROWS, cs]
                acc = acc + u * wcol[j:j + 1, :]
            c_scr[r0:r0 + CONV_ROWS, cs] = acc

    c = c_scr[...]
    cm = jnp.mean(c, axis=-1, keepdims=True)
    d = c - cm
    cv = jnp.mean(d * d, axis=-1, keepdims=True)
    y = d * lax.rsqrt(cv + LN_EPS) * lnw_ref[...] + lnb_ref[...]
    o_ref[...] = (y * jax.nn.sigmoid(y)).astype(o_ref.dtype)


def _conv(u, seq_len, p, tb=512):
    n = u.shape[0]
    hb = tb // HALO
    nhb = n // HALO
    vec = pl.BlockSpec((1, D_CONV), lambda i: (0, 0))
    return pl.pallas_call(
        functools.partial(_conv_kernel, seq_len, tb),
        grid=(n // tb,),
        in_specs=[
            pl.BlockSpec((tb, D_CONV), lambda i: (i, 0)),
            pl.BlockSpec((HALO, D_CONV), lambda i: (jnp.maximum(i * hb - 1, 0), 0)),
            pl.BlockSpec((HALO, D_CONV), lambda i: (jnp.minimum((i + 1) * hb, nhb - 1), 0)),
            pl.BlockSpec((CONV_WIDTH, D_CONV), lambda i: (0, 0)),
            vec, vec, vec,
        ],
        out_specs=pl.BlockSpec((tb, D_CONV), lambda i: (i, 0)),
        out_shape=jax.ShapeDtypeStruct((n, D_CONV), BF16),
        scratch_shapes=[
            pltpu.VMEM((tb + 2 * HALO, D_CONV), F32),
            pltpu.VMEM((SUBLANES - 1, tb + 2 * HALO - SUBLANES, D_CONV), F32),
            pltpu.VMEM((tb, D_CONV), F32),
        ],
        compiler_params=_cparams(("parallel",)),
        name="conv_module",
    )(u, u, u, p["dw_w"], p["dw_b"], p["cln_w"], p["cln_b"])


OUTPROJ_ROWS = 256


def _outproj_kernel(n_lo, orw_ref, ocv_ref, xp_ref, xs_ref, w_ref, gpost_ref, gpre_ref, x1_ref, hm_ref):
    from_prompt = pl.program_id(0) < n_lo
    tm = x1_ref.shape[0]
    for r0 in range(0, tm, OUTPROJ_ROWS):
        rs = slice(r0, r0 + OUTPROJ_ROWS)
        mix = _dot(orw_ref[rs, :], w_ref[0:D_RWKV, :]) + _dot(ocv_ref[rs, :], w_ref[D_RWKV:D_MODEL, :])
        x = jnp.where(from_prompt, xp_ref[rs, :], xs_ref[rs, :])
        x1 = x + _rms_rows(mix, gpost_ref[...])
        x1_ref[rs, :] = x1
        hm_ref[rs, :] = _rms_rows(x1, gpre_ref[...]).astype(hm_ref.dtype)


def _outproj(o_rwkv, o_conv, xp, xs, p, tm=512):
    n = o_rwkv.shape[0]
    n_lo = xp.shape[0] // tm
    half = pl.BlockSpec((tm, D_RWKV), lambda i: (i, 0))
    row = pl.BlockSpec((tm, D_MODEL), lambda i: (i, 0))
    vec = pl.BlockSpec((1, D_MODEL), lambda i: (0, 0))
    return pl.pallas_call(
        functools.partial(_outproj_kernel, n_lo),
        grid=(n // tm,),
        in_specs=[half, half, _lo_spec(tm, D_MODEL, n_lo), _hi_spec(tm, D_MODEL, n_lo),
                  pl.BlockSpec((D_MODEL, D_MODEL), lambda i: (0, 0)), vec, vec],
        out_specs=[row, row],
        out_shape=[jax.ShapeDtypeStruct((n, D_MODEL), F32), jax.ShapeDtypeStruct((n, D_MODEL), BF16)],
        compiler_params=_cparams(("parallel",)),
        name="outproj",
    )(o_rwkv, o_conv, xp, xs, p["w_out"], p["g_post_mix"], p["g_pre_mlp"])


def _mlp_kernel(n_lo, hm_ref, x1_ref, wup_ref, wdn_ref, g_ref, op_ref, os_ref, acc_ref):
    i = pl.program_id(0)
    j = pl.program_id(1)
    @pl.when(j == 0)
    def _():
        acc_ref[...] = jnp.zeros_like(acc_ref)

    h = _dot(hm_ref[...], wup_ref[...])
    h = jnp.square(jnp.maximum(h, 0.0)).astype(BF16)
    acc_ref[...] += _dot(h, wdn_ref[...])

    def finish(o_ref):
        o_ref[...] = x1_ref[...] + _rms_rows(acc_ref[...], g_ref[...])

    is_last = j == pl.num_programs(1) - 1
    pl.when(is_last & (i < n_lo))(lambda: finish(op_ref))
    pl.when(is_last & (i >= n_lo))(lambda: finish(os_ref))


def _mlp(hm, x1, n_prompt, p, tm=512, tf=1024):
    n = hm.shape[0]
    n_lo = n_prompt // tm
    row = pl.BlockSpec((tm, D_MODEL), lambda i, j: (i, 0))
    return pl.pallas_call(
        functools.partial(_mlp_kernel, n_lo),
        grid=(n // tm, D_FF // tf),
        in_specs=[
            row, row,
            pl.BlockSpec((D_MODEL, tf), lambda i, j: (0, j)),
            pl.BlockSpec((tf, D_MODEL), lambda i, j: (j, 0)),
            pl.BlockSpec((1, D_MODEL), lambda i, j: (0, 0)),
        ],
        out_specs=[_lo_spec(tm, D_MODEL, n_lo), _hi_spec(tm, D_MODEL, n_lo)],
        out_shape=[jax.ShapeDtypeStruct((n_prompt, D_MODEL), F32),
                   jax.ShapeDtypeStruct((n - n_prompt, D_MODEL), F32)],
        scratch_shapes=[pltpu.VMEM((tm, D_MODEL), F32)],
        compiler_params=_cparams(("arbitrary", "arbitrary")),
        name="mlp",
    )(hm, x1, p["w_up"], p["w_down"], p["g_post_mlp"])


def _pad_rows(w, rows, offset):
    out = jnp.zeros((rows, w.shape[1]), w.dtype)
    return out.at[offset:offset + w.shape[0]].set(w)


def _layer_params(l, g_pre_mix, w_in, mu_prev, mu_next, w0_f, w2_f, w0_b, w2_b, a0_f, a2_f, a0_b, a2_b,
                  g2, k_k, k_a, r_k, gn_w, gn_b, dw_w, dw_b, cln_w, cln_b, w_out, g_post_mix,
                  g_pre_mlp, w_up, w_down, g_post_mlp):
    rwkv_cols = 3 * D_RWKV + LORA_COLS
    row = lambda a: a[l].reshape(1, -1).astype(F32)
    pad_cols = lambda a, width: jnp.pad(a, ((0, 0), (0, width - a.shape[1])))
    w = (g_pre_mix[l].astype(F32)[:, None] * w_in[l]).astype(BF16)
    heads = jnp.arange(D_RWKV) // HEAD_DIM
    return dict(
        w_in=w, w_zc=w[:, rwkv_cols:],
        mu_prev=pad_cols(row(mu_prev), ZR_COLS), mu_next=pad_cols(row(mu_next), ZR_COLS),
        w2=jnp.concatenate([_pad_rows(w2_f[l], 128, 0), _pad_rows(w2_b[l], 128, DECAY_LORA)], axis=1).astype(BF16),
        a2=jnp.concatenate([_pad_rows(a2_f[l], 128, 0), _pad_rows(a2_b[l], 128, AAA_LORA)], axis=1).astype(BF16),
        g2=_pad_rows(g2[l], 256, 0).astype(BF16),
        w0=jnp.concatenate([row(w0_f), row(w0_b)], axis=1),
        a0=jnp.concatenate([row(a0_f), row(a0_b)], axis=1),
        k_k=row(k_k), k_a=row(k_a), r_k=row(r_k), gn_w=row(gn_w), gn_b=row(gn_b),
        e=(heads[:, None] == heads[None, :]).astype(BF16),
        dw_w=dw_w[l].astype(F32), dw_b=row(dw_b), cln_w=row(cln_w), cln_b=row(cln_b),
        w_out=w_out[l].astype(BF16), g_post_mix=row(g_post_mix), g_pre_mlp=row(g_pre_mlp),
        w_up=w_up[l].astype(BF16), w_down=w_down[l].astype(BF16), g_post_mlp=row(g_post_mlp),
    )


def _layer(xp, xs, seq_len, p):
    n_seq = (xp.shape[0] + xs.shape[0]) // seq_len
    zr = _inproj(xp, xs, p["w_in"], "inproj_rwkv", cols=ZR_COLS)
    u_glu = _inproj(xp, xs, p["w_zc"], "inproj_conv", glu=True)
    prep = _prep(zr, seq_len, p)
    g, bonus = prep[9], prep[10]
    yf, yb = _wkv(prep[:9], n_seq, seq_len)
    o_rwkv = _post(yf, yb, g, bonus, p)
    o_conv = _conv(u_glu, seq_len, p)
    x1, hm = _outproj(o_rwkv, o_conv, xp, xs, p)
    return _mlp(hm, x1, xp.shape[0], p)


def _forward(x_prompt, x_sample, weights):
    seq_len = x_prompt.shape[1]
    assert x_sample.shape[1] == seq_len
    xp = x_prompt.reshape(-1, D_MODEL)
    xs = x_sample.reshape(-1, D_MODEL)
    depth = weights[0].shape[0]
    for l in range(depth):
        xp, xs = _layer(xp, xs, seq_len, _layer_params(l, *weights))
    return xp.reshape(x_prompt.shape), xs.reshape(x_sample.shape)


def kernel(x_prompt, x_sample, g_pre_mix, w_in, mu_prev, mu_next, w0_f, w2_f, w0_b, w2_b, a0_f, a2_f, a0_b, a2_b, g2, k_k, k_a, r_k, gn_w, gn_b, dw_w, dw_b, cln_w, cln_b, w_out, g_post_mix, g_pre_mlp, w_up, w_down, g_post_mlp):
    weights = (g_pre_mix, w_in, mu_prev, mu_next, w0_f, w2_f, w0_b, w2_b, a0_f, a2_f, a0_b, a2_b,
               g2, k_k, k_a, r_k, gn_w, gn_b, dw_w, dw_b, cln_w, cln_b, w_out, g_post_mix,
               g_pre_mlp, w_up, w_down, g_post_mlp)
    return _forward(x_prompt, x_sample, weights)
```

```python
import functools
import math

import jax
import jax.numpy as jnp
from jax import lax
from jax.experimental import pallas as pl
from jax.experimental.pallas import tpu as pltpu

F32 = jnp.float32
BF16 = jnp.bfloat16

D_MODEL = 2048
D_RWKV = 1024
D_CONV = 1024
HEAD_DIM = 64
N_HEADS = 16
DECAY_LORA = 64
AAA_LORA = 64
GATE_LORA = 160
CONV_WIDTH = 31
CONV_HALF = CONV_WIDTH // 2
D_FF = 4 * D_MODEL
RMS_EPS = 1e-6
LN_EPS = 1e-5
GN_EPS = 64e-5
L2_EPS = 1e-12

LORA_COLS = 2 * DECAY_LORA + 2 * AAA_LORA + GATE_LORA
LORA_PAD = 512
ZR_COLS = 3 * D_RWKV + LORA_PAD

CHUNK = 64
GROUP = 256
N_GROUPS = D_RWKV // GROUP
WAVE_LAG = 3
WAVES_PER_ITER = 4
HALO = 16
SUBLANES = 8
LANES = 128

VMEM_LIMIT = 56 * 1024 * 1024


def _cparams(sem):
    return pltpu.CompilerParams(dimension_semantics=sem, vmem_limit_bytes=VMEM_LIMIT)


def _dot(a, b):
    return jnp.dot(a, b, preferred_element_type=F32)


def _dot_nt(a, b):
    return lax.dot_general(a, b, (((1,), (1,)), ((), ())), preferred_element_type=F32)


def _dot_tn(a, b):
    return lax.dot_general(a, b, (((0,), (0,)), ((), ())), preferred_element_type=F32)


def _split2(x):
    hi = x.astype(BF16)
    lo = (x - hi.astype(F32)).astype(BF16)
    return hi, lo


def _seg_sum(x, e_ref):
    return _dot(x.astype(BF16), e_ref[...])


def _rms_rows(x, g):
    ms = jnp.mean(x * x, axis=-1, keepdims=True)
    return x * lax.rsqrt(ms + RMS_EPS) * g


def _lo_spec(tm, width, n_lo):
    return pl.BlockSpec((tm, width), lambda i, *_: (jnp.minimum(i, n_lo - 1), 0))


def _hi_spec(tm, width, n_lo):
    return pl.BlockSpec((tm, width), lambda i, *_: (jnp.maximum(i - n_lo, 0), 0))


def _inproj_kernel(n_lo, tn, glu, xp_ref, xs_ref, w_ref, z_ref):
    def run(x_ref):
        x = x_ref[...]
        rinv = lax.rsqrt(jnp.mean(x * x, axis=-1, keepdims=True) + RMS_EPS)
        xb = x.astype(BF16)
        out_cols = z_ref.shape[1]
        for c in range(out_cols // tn):
            cs = slice(c * tn, (c + 1) * tn)
            z = _dot(xb, w_ref[:, cs]) * rinv
            if glu:
                gate = _dot(xb, w_ref[:, out_cols + c * tn:out_cols + (c + 1) * tn]) * rinv
                z = z * jax.nn.sigmoid(gate)
            z_ref[:, cs] = z.astype(z_ref.dtype)

    i = pl.program_id(0)
    pl.when(i < n_lo)(lambda: run(xp_ref))
    pl.when(i >= n_lo)(lambda: run(xs_ref))


def _inproj(xp, xs, w, name, cols=None, glu=False, tm=512, tn=512):
    n_lo = xp.shape[0] // tm
    n = xp.shape[0] + xs.shape[0]
    cols = w.shape[1] if cols is None else cols
    out_cols = cols // 2 if glu else cols
    return pl.pallas_call(
        functools.partial(_inproj_kernel, n_lo, tn, glu),
        grid=(n // tm,),
        in_specs=[
            _lo_spec(tm, D_MODEL, n_lo),
            _hi_spec(tm, D_MODEL, n_lo),
            pl.BlockSpec((D_MODEL, cols), lambda i: (0, 0), pipeline_mode=pl.Buffered(1)),
        ],
        out_specs=pl.BlockSpec((tm, out_cols), lambda i: (i, 0)),
        out_shape=jax.ShapeDtypeStruct((n, out_cols), BF16),
        compiler_params=_cparams(("parallel",)),
        name=name,
    )(xp, xs, w)


def _prep_kernel(seq_len, tb,
                 z_ref, zp_ref, zn_ref, mup_ref, mun_ref, w2_ref, a2_ref, g2_ref,
                 w0_ref, a0_ref, kk_w_ref, ka_ref, rk_ref, e_ref,
                 r_ref, v_ref, kk_ref, lwf_ref, kmf_ref, bf_ref, lwb_ref, kmb_ref, bb_ref,
                 g_ref, bonus_ref):
    i = pl.program_id(0)
    t0 = (i * tb) % seq_len
    first = t0 == 0
    last = (t0 + tb) == seq_len
    edge = lax.broadcasted_iota(jnp.int32, (SUBLANES, 1), 0)

    def shifted(c0, c1):
        z = z_ref[:, c0:c1].astype(F32)
        mp = mup_ref[:, c0:c1]
        mn = mun_ref[:, c0:c1]
        body = (1.0 - mp - mn) * z + mp * pltpu.roll(z, 1, axis=0) + mn * pltpu.roll(z, tb - 1, axis=0)
        prev_row = jnp.where(first, 0.0, zp_ref[HALO - 1:HALO, c0:c1].astype(F32))
        next_row = jnp.where(last, 0.0, zn_ref[0:1, c0:c1].astype(F32))
        top = body[0:SUBLANES] + jnp.where(edge == 0, mp * (prev_row - z[tb - 1:tb]), 0.0)
        bot = body[tb - SUBLANES:tb] + jnp.where(edge == SUBLANES - 1, mn * (next_row - z[0:1]), 0.0)
        return jnp.concatenate([top, body[SUBLANES:tb - SUBLANES], bot], axis=0)

    r = shifted(0, D_RWKV)
    k = shifted(D_RWKV, 2 * D_RWKV)
    v = shifted(2 * D_RWKV, 3 * D_RWKV)
    lora = shifted(3 * D_RWKV, ZR_COLS)

    xw = jnp.tanh(lora[:, 0:128]).astype(BF16)
    xa = lora[:, 128:256].astype(BF16)
    xg = jax.nn.sigmoid(lora[:, 256:512]).astype(BF16)
    dw = _dot(xw, w2_ref[...]) + w0_ref[...]
    da = _dot(xa, a2_ref[...]) + a0_ref[...]
    g_ref[...] = _dot(xg, g2_ref[...]).astype(g_ref.dtype)

    kraw = k * kk_w_ref[...]
    ss = _seg_sum(kraw * kraw, e_ref)
    kk = kraw * lax.rsqrt(jnp.maximum(ss, L2_EPS * L2_EPS))
    rk = _seg_sum(r * k * rk_ref[...], e_ref)
    bonus_ref[...] = (rk * v).astype(bonus_ref.dtype)
    r_ref[...] = r.astype(r_ref.dtype)
    v_ref[...] = v.astype(v_ref.dtype)
    kk_ref[...] = kk.astype(kk_ref.dtype)

    neg_c = -math.exp(-0.5)
    ka = ka_ref[...]
    for d, (lw_ref, km_ref, b_ref) in enumerate(((lwf_ref, kmf_ref, bf_ref), (lwb_ref, kmb_ref, bb_ref))):
        sl = slice(d * D_RWKV, (d + 1) * D_RWKV)
        lw_ref[...] = neg_c * jax.nn.sigmoid(dw[:, sl])
        a = jax.nn.sigmoid(da[:, sl])
        km_ref[...] = (k * (1.0 + (a - 1.0) * ka)).astype(km_ref.dtype)
        b_ref[...] = (kk * a).astype(b_ref.dtype)


_PREP_OUT_DTYPES = (BF16, BF16, BF16, F32, BF16, BF16, F32, BF16, BF16, BF16, BF16)


def _prep(zr, seq_len, p, tb=256):
    n = zr.shape[0]
    hb = tb // HALO
    nhb = n // HALO
    vec = lambda w: pl.BlockSpec((1, w), lambda i: (0, 0))
    full = lambda a: pl.BlockSpec(a.shape, lambda i: (0, 0))
    out_spec = pl.BlockSpec((tb, D_RWKV), lambda i: (i, 0))
    return pl.pallas_call(
        functools.partial(_prep_kernel, seq_len, tb),
        grid=(n // tb,),
        in_specs=[
            pl.BlockSpec((tb, ZR_COLS), lambda i: (i, 0)),
            pl.BlockSpec((HALO, ZR_COLS), lambda i: (jnp.maximum(i * hb - 1, 0), 0)),
            pl.BlockSpec((HALO, ZR_COLS), lambda i: (jnp.minimum((i + 1) * hb, nhb - 1), 0)),
            vec(ZR_COLS), vec(ZR_COLS),
            full(p["w2"]), full(p["a2"]), full(p["g2"]),
            vec(2 * D_RWKV), vec(2 * D_RWKV), vec(D_RWKV), vec(D_RWKV), vec(D_RWKV),
            full(p["e"]),
        ],
        out_specs=[out_spec] * 11,
        out_shape=[jax.ShapeDtypeStruct((n, D_RWKV), dt) for dt in _PREP_OUT_DTYPES],
        compiler_params=_cparams(("parallel",)),
        name="rwkv_prep",
    )(zr, zr, zr, p["mu_prev"], p["mu_next"], p["w2"], p["a2"], p["g2"],
      p["w0"], p["a0"], p["k_k"], p["k_a"], p["r_k"], p["e"])


def _wkv_masks(reverse):
    t = lax.broadcasted_iota(jnp.int32, (CHUNK, GROUP), 0)
    lane = lax.broadcasted_iota(jnp.int32, (CHUNK, GROUP), 1)
    s = lane % CHUNK
    rr = lax.broadcasted_iota(jnp.int32, (LANES, LANES), 0)
    cc = lax.broadcasted_iota(jnp.int32, (LANES, LANES), 1)
    ti = lax.broadcasted_iota(jnp.int32, (CHUNK, CHUNK), 0)
    si = lax.broadcasted_iota(jnp.int32, (CHUNK, CHUNK), 1)
    return dict(
        strict=(s > t) if reverse else (s < t),
        incl=(s >= t) if reverse else (s <= t),
        eye=(s == t).astype(F32),
        bd=(rr // HEAD_DIM) == (cc // HEAD_DIM),
        first_head=lax.broadcasted_iota(jnp.int32, (CHUNK, LANES), 1) < HEAD_DIM,
        tri=((si >= ti) if reverse else (si <= ti)).astype(BF16),
    )


def _bd(x, m):
    zero = jnp.zeros((), x.dtype)
    quad = lambda h: jnp.where(m["bd"], jnp.concatenate([x[:, h * LANES:(h + 1) * LANES]] * 2, axis=0), zero)
    zq = jnp.zeros((LANES, LANES), x.dtype)
    return jnp.concatenate([jnp.concatenate([quad(0), zq], axis=0),
                            jnp.concatenate([zq, quad(1)], axis=0)], axis=1)


def _diag_blocks(full, m):
    halves = []
    for half in range(GROUP // LANES):
        lanes = slice(half * LANES, (half + 1) * LANES)
        r0 = half * LANES
        halves.append(jnp.where(m["first_head"], full[r0:r0 + CHUNK, lanes], full[r0 + CHUNK:r0 + LANES, lanes]))
    return jnp.concatenate(halves, axis=1)


def _wkv_stage0(refs, off, reverse, m):
    r_ref, v_ref, kk_ref, lw_ref, km_ref, b_ref = refs
    sl = pl.ds(off, CHUNK)
    lw = lw_ref[sl, :]
    hi, lo = _split2(lw)
    tri = m["tri"]
    cum = _dot(tri, hi) + _dot(tri, lo)
    tot = cum[0:1, :] if reverse else cum[CHUNK - 1:CHUNK, :]
    e_neg = jnp.exp(-cum)
    e_rest = jnp.exp(tot - cum)
    b = b_ref[sl, :].astype(F32)
    km = km_ref[sl, :].astype(F32)
    return dict(
        a_t=-kk_ref[sl, :].astype(F32) * jnp.exp(cum - lw),
        r_t=r_ref[sl, :].astype(F32) * jnp.exp(cum),
        b_h=b * e_neg, k_h=km * e_neg, b_r=b * e_rest, k_r=km * e_rest,
        v=v_ref[sl, :], e_tot=jnp.exp(tot),
    )


def _wkv_wave(fwd_refs, bwd_refs, off_f, off_b, mf, mb, yf_ref, yb_ref, sf_ref, sb_ref):
    pre_f = _wkv_stage0(fwd_refs, off_f, False, mf)
    pre_b = _wkv_stage0(bwd_refs, off_b, True, mb)
    yield
    insts = []
    for g in range(N_GROUPS):
        insts.append(dict(pre=pre_f, g=g, m=mf, s_ref=sf_ref, y_ref=yf_ref, sl=pl.ds(off_f, CHUNK)))
        insts.append(dict(pre=pre_b, g=g, m=mb, s_ref=sb_ref, y_ref=yb_ref, sl=pl.ds(off_b, CHUNK)))
    rng = range(len(insts))
    ms = [it["m"] for it in insts]
    ls = [slice(it["g"] * GROUP, (it["g"] + 1) * GROUP) for it in insts]
    pre = [it["pre"] for it in insts]
    rt = [pre[i]["r_t"][:, ls[i]] for i in rng]
    at_b = [pre[i]["a_t"][:, ls[i]].astype(BF16) for i in rng]
    v_b = [pre[i]["v"][:, ls[i]].astype(BF16) for i in rng]
    lhs = [jnp.concatenate([at_b[i], rt[i].astype(BF16)], axis=0) for i in rng]
    gb = [_dot_nt(lhs[i], _bd(pre[i]["b_h"][:, ls[i]].astype(BF16), ms[i])) for i in rng]
    yield
    gk = [_dot_nt(lhs[i], _bd(pre[i]["k_h"][:, ls[i]].astype(BF16), ms[i])) for i in rng]
    a_ab = [jnp.where(ms[i]["strict"], gb[i][:CHUNK], 0.0) for i in rng]
    a_rb = [jnp.where(ms[i]["incl"], gb[i][CHUNK:], 0.0).astype(BF16) for i in rng]
    yield
    t_inv = [ms[i]["eye"] + a_ab[i] for i in rng]
    xb = [a.astype(BF16) for a in a_ab]
    x = [_dot(xb[i], _bd(xb[i], ms[i])) for i in rng]
    a_k = [jnp.concatenate([jnp.where(ms[i]["strict"], gk[i][:CHUNK], 0.0),
                            jnp.where(ms[i]["incl"], gk[i][CHUNK:], 0.0)], axis=0).astype(BF16) for i in rng]
    av = [_dot(a_k[i], _bd(v_b[i], ms[i])) for i in rng]
    yield
    for _ in range(int(math.log2(CHUNK)) - 2):
        xb = [xi.astype(BF16) for xi in x]
        res = [_dot(jnp.concatenate([xb[i], t_inv[i].astype(BF16)], axis=0), _bd(xb[i], ms[i])) for i in rng]
        x = [r[:CHUNK] for r in res]
        t_inv = [t_inv[i] + res[i][CHUNK:] for i in rng]
        yield
    res = [_dot(t_inv[i].astype(BF16), _bd(x[i].astype(BF16), ms[i])) for i in rng]
    t_b = [(t_inv[i] + res[i]).astype(BF16) for i in rng]
    bk_b = [jnp.concatenate([pre[i]["b_r"][:, ls[i]], pre[i]["k_r"][:, ls[i]]], axis=0).astype(BF16) for i in rng]
    ard = [jnp.concatenate([at_b[i], rt[i].astype(BF16),
                            (ms[i]["eye"] * pre[i]["e_tot"][:, ls[i]]).astype(BF16)], axis=0) for i in rng]
    yield
    res = [_dot(ard[i], _bd(insts[i]["s_ref"][insts[i]["g"]].astype(BF16), ms[i])) for i in rng]
    yield
    u_b = [_dot(t_b[i], _bd((res[i][:CHUNK] + av[i][:CHUNK]).astype(BF16), ms[i])).astype(BF16) for i in rng]
    yield
    y = [res[i][CHUNK:2 * CHUNK] + av[i][CHUNK:] + _dot(a_rb[i], _bd(u_b[i], ms[i])) for i in rng]
    upd = [_dot_tn(bk_b[i], jnp.concatenate([u_b[i], v_b[i]], axis=0)) for i in rng]
    for i in rng:
        it = insts[i]
        it["y_ref"][it["sl"], ls[i]] = y[i].astype(it["y_ref"].dtype)
        it["s_ref"][it["g"]] = res[i][2 * CHUNK:] + _diag_blocks(upd[i], ms[i])


def _interleave(gens, lag):
    pending = list(gens)
    active = []
    tick = 0
    while pending or active:
        if pending and tick % lag == 0:
            active.append(pending.pop(0))
        for g in list(active):
            if next(g, StopIteration) is StopIteration:
                active.remove(g)
        tick += 1


def _wkv_kernel(n_chunks, *refs):
    fwd_refs = refs[0:6]
    bwd_refs = refs[6:12]
    yf_ref, yb_ref, sf_ref, sb_ref = refs[12:16]

    @pl.when(pl.program_id(1) == 0)
    def _():
        sf_ref[...] = jnp.zeros_like(sf_ref)
        sb_ref[...] = jnp.zeros_like(sb_ref)

    mf = _wkv_masks(False)
    mb = _wkv_masks(True)
    def body(it, carry):
        waves = []
        for k in range(WAVES_PER_ITER):
            c = it * WAVES_PER_ITER + k
            waves.append(_wkv_wave(fwd_refs, bwd_refs, pl.multiple_of(c * CHUNK, CHUNK),
                                   pl.multiple_of((n_chunks - 1 - c) * CHUNK, CHUNK), mf, mb,
                                   yf_ref, yb_ref, sf_ref, sb_ref))
        _interleave(waves, WAVE_LAG)
        return carry

    lax.fori_loop(0, n_chunks // WAVES_PER_ITER, body, 0)


def _wkv(pp, n_seq, seq_len, tb=512):
    (r, v, kk, lwf, kmf, bf, lwb, kmb, bb) = pp
    n = r.shape[0]
    nb = seq_len // tb
    fspec = pl.BlockSpec((tb, D_RWKV), lambda s, i: (s * nb + i, 0))
    bspec = pl.BlockSpec((tb, D_RWKV), lambda s, i: (s * nb + nb - 1 - i, 0))
    return pl.pallas_call(
        functools.partial(_wkv_kernel, tb // CHUNK),
        grid=(n_seq, nb),
        in_specs=[fspec] * 6 + [bspec] * 6,
        out_specs=[fspec, bspec],
        out_shape=[jax.ShapeDtypeStruct((n, D_RWKV), BF16)] * 2,
        scratch_shapes=[pltpu.VMEM((N_GROUPS, CHUNK, GROUP), F32)] * 2,
        compiler_params=_cparams(("parallel", "arbitrary")),
        name="wkv",
    )(r, v, kk, lwf, kmf, bf, r, v, kk, lwb, kmb, bb)


def _post_kernel(yf_ref, yb_ref, g_ref, bonus_ref, gnw_ref, gnb_ref, e_ref, o_ref):
    y = yf_ref[...].astype(F32) + yb_ref[...].astype(F32)
    mu = _seg_sum(y, e_ref) * (1.0 / HEAD_DIM)
    d = y - mu
    var = _seg_sum(d * d, e_ref) * (1.0 / HEAD_DIM)
    yn = d * lax.rsqrt(var + GN_EPS) * gnw_ref[...] + gnb_ref[...]
    o_ref[...] = ((yn + bonus_ref[...].astype(F32)) * g_ref[...].astype(F32)).astype(o_ref.dtype)


def _post(yf, yb, g, bonus, p, tb=512):
    n = yf.shape[0]
    spec = pl.BlockSpec((tb, D_RWKV), lambda i: (i, 0))
    vec = pl.BlockSpec((1, D_RWKV), lambda i: (0, 0))
    return pl.pallas_call(
        _post_kernel,
        grid=(n // tb,),
        in_specs=[spec] * 4 + [vec, vec, pl.BlockSpec(p["e"].shape, lambda i: (0, 0))],
        out_specs=spec,
        out_shape=jax.ShapeDtypeStruct((n, D_RWKV), BF16),
        compiler_params=_cparams(("parallel",)),
        name="rwkv_post",
    )(yf, yb, g, bonus, p["gn_w"], p["gn_b"], p["e"])


CONV_ROWS = 64


def _conv_kernel(seq_len, tb, u_ref, up_ref, un_ref, dww_ref, dwb_ref, lnw_ref, lnb_ref,
                 o_ref, u_scr, sh_scr, c_scr):
    i = pl.program_id(0)
    t0 = (i * tb) % seq_len
    first = t0 == 0
    last = (t0 + tb) == seq_len

    u_scr[HALO:HALO + tb, :] = u_ref[...].astype(F32)
    u_scr[0:HALO, :] = jnp.where(first, 0.0, up_ref[...].astype(F32))
    u_scr[HALO + tb:2 * HALO + tb, :] = jnp.where(last, 0.0, un_ref[...].astype(F32))

    sh_rows = tb + 2 * HALO - SUBLANES
    for s in range(1, SUBLANES):
        sh_scr[s - 1] = u_scr[s:s + sh_rows, :]

    base = HALO - CONV_HALF
    for cb in range(D_CONV // LANES):
        cs = slice(cb * LANES, (cb + 1) * LANES)
        wcol = dww_ref[:, cs]
        bias = dwb_ref[:, cs]
        for rc in range(tb // CONV_ROWS):
            r0 = rc * CONV_ROWS
            acc = jnp.zeros((CONV_ROWS, LANES), F32) + bias
            for j in range(CONV_WIDTH):
                s = (base + j) % SUBLANES
                q = r0 + base + j - s
                if s == 0:
                    u = u_scr[q:q + CONV_ROWS, cs]
                else:
                    u = sh_scr[s - 1, q:q + CONV_ROWS, cs]
                acc = acc + u * wcol[j:j + 1, :]
            c_scr[r0:r0 + CONV_ROWS, cs] = acc

    c = c_scr[...]
    cm = jnp.mean(c, axis=-1, keepdims=True)
    d = c - cm
    cv = jnp.mean(d * d, axis=-1, keepdims=True)
    y = d * lax.rsqrt(cv + LN_EPS) * lnw_ref[...] + lnb_ref[...]
    o_ref[...] = (y * jax.nn.sigmoid(y)).astype(o_ref.dtype)


def _conv(u, seq_len, p, tb=256):
    n = u.shape[0]
    hb = tb // HALO
    nhb = n // HALO
    vec = pl.BlockSpec((1, D_CONV), lambda i: (0, 0))
    return pl.pallas_call(
        functools.partial(_conv_kernel, seq_len, tb),
        grid=(n // tb,),
        in_specs=[
            pl.BlockSpec((tb, D_CONV), lambda i: (i, 0)),
            pl.BlockSpec((HALO, D_CONV), lambda i: (jnp.maximum(i * hb - 1, 0), 0)),
            pl.BlockSpec((HALO, D_CONV), lambda i: (jnp.minimum((i + 1) * hb, nhb - 1), 0)),
            pl.BlockSpec((CONV_WIDTH, D_CONV), lambda i: (0, 0)),
            vec, vec, vec,
        ],
        out_specs=pl.BlockSpec((tb, D_CONV), lambda i: (i, 0)),
        out_shape=jax.ShapeDtypeStruct((n, D_CONV), BF16),
        scratch_shapes=[
            pltpu.VMEM((tb + 2 * HALO, D_CONV), F32),
            pltpu.VMEM((SUBLANES - 1, tb + 2 * HALO - SUBLANES, D_CONV), F32),
            pltpu.VMEM((tb, D_CONV), F32),
        ],
        compiler_params=_cparams(("parallel",)),
        name="conv_module",
    )(u, u, u, p["dw_w"], p["dw_b"], p["cln_w"], p["cln_b"])


OUTPROJ_ROWS = 256


def _outproj_kernel(n_lo, orw_ref, ocv_ref, xp_ref, xs_ref, w_ref, gpost_ref, gpre_ref, x1_ref, hm_ref):
    from_prompt = pl.program_id(0) < n_lo
    tm = x1_ref.shape[0]
    for r0 in range(0, tm, OUTPROJ_ROWS):
        rs = slice(r0, r0 + OUTPROJ_ROWS)
        mix = _dot(orw_ref[rs, :], w_ref[0:D_RWKV, :]) + _dot(ocv_ref[rs, :], w_ref[D_RWKV:D_MODEL, :])
        x = jnp.where(from_prompt, xp_ref[rs, :], xs_ref[rs, :])
        x1 = x + _rms_rows(mix, gpost_ref[...])
        x1_ref[rs, :] = x1
        hm_ref[rs, :] = _rms_rows(x1, gpre_ref[...]).astype(hm_ref.dtype)


def _outproj(o_rwkv, o_conv, xp, xs, p, tm=512):
    n = o_rwkv.shape[0]
    n_lo = xp.shape[0] // tm
    half = pl.BlockSpec((tm, D_RWKV), lambda i: (i, 0))
    row = pl.BlockSpec((tm, D_MODEL), lambda i: (i, 0))
    vec = pl.BlockSpec((1, D_MODEL), lambda i: (0, 0))
    return pl.pallas_call(
        functools.partial(_outproj_kernel, n_lo),
        grid=(n // tm,),
        in_specs=[half, half, _lo_spec(tm, D_MODEL, n_lo), _hi_spec(tm, D_MODEL, n_lo),
                  pl.BlockSpec((D_MODEL, D_MODEL), lambda i: (0, 0)), vec, vec],
        out_specs=[row, row],
        out_shape=[jax.ShapeDtypeStruct((n, D_MODEL), F32), jax.ShapeDtypeStruct((n, D_MODEL), BF16)],
        compiler_params=_cparams(("parallel",)),
        name="outproj",
    )(o_rwkv, o_conv, xp, xs, p["w_out"], p["g_post_mix"], p["g_pre_mlp"])


def _mlp_kernel(n_lo, hm_ref, x1_ref, wup_ref, wdn_ref, g_ref, op_ref, os_ref, acc_ref):
    i = pl.program_id(0)
    j = pl.program_id(1)
    @pl.when(j == 0)
    def _():
        acc_ref[...] = jnp.zeros_like(acc_ref)

    h = _dot(hm_ref[...], wup_ref[...])
    h = jnp.square(jnp.maximum(h, 0.0)).astype(BF16)
    acc_ref[...] += _dot(h, wdn_ref[...])

    def finish(o_ref):
        o_ref[...] = x1_ref[...] + _rms_rows(acc_ref[...], g_ref[...])

    is_last = j == pl.num_programs(1) - 1
    pl.when(is_last & (i < n_lo))(lambda: finish(op_ref))
    pl.when(is_last & (i >= n_lo))(lambda: finish(os_ref))


def _mlp(hm, x1, n_prompt, p, tm=512, tf=1024):
    n = hm.shape[0]
    n_lo = n_prompt // tm
    row = pl.BlockSpec((tm, D_MODEL), lambda i, j: (i, 0))
    return pl.pallas_call(
        functools.partial(_mlp_kernel, n_lo),
        grid=(n // tm, D_FF // tf),
        in_specs=[
            row, row,
            pl.BlockSpec((D_MODEL, tf), lambda i, j: (0, j)),
            pl.BlockSpec((tf, D_MODEL), lambda i, j: (j, 0)),
            pl.BlockSpec((1, D_MODEL), lambda i, j: (0, 0)),
        ],
        out_specs=[_lo_spec(tm, D_MODEL, n_lo), _hi_spec(tm, D_MODEL, n_lo)],
        out_shape=[jax.ShapeDtypeStruct((n_prompt, D_MODEL), F32),
                   jax.ShapeDtypeStruct((n - n_prompt, D_MODEL), F32)],
        scratch_shapes=[pltpu.VMEM((tm, D_MODEL), F32)],
        compiler_params=_cparams(("arbitrary", "arbitrary")),
        name="mlp",
    )(hm, x1, p["w_up"], p["w_down"], p["g_post_mlp"])


def _pad_rows(w, rows, offset):
    out = jnp.zeros((rows, w.shape[1]), w.dtype)
    return out.at[offset:offset + w.shape[0]].set(w)


def _layer_params(l, g_pre_mix, w_in, mu_prev, mu_next, w0_f, w2_f, w0_b, w2_b, a0_f, a2_f, a0_b, a2_b,
                  g2, k_k, k_a, r_k, gn_w, gn_b, dw_w, dw_b, cln_w, cln_b, w_out, g_post_mix,
                  g_pre_mlp, w_up, w_down, g_post_mlp):
    rwkv_cols = 3 * D_RWKV + LORA_COLS
    row = lambda a: a[l].reshape(1, -1).astype(F32)
    pad_cols = lambda a, width: jnp.pad(a, ((0, 0), (0, width - a.shape[1])))
    w = (g_pre_mix[l].astype(F32)[:, None] * w_in[l]).astype(BF16)
    heads = jnp.arange(D_RWKV) // HEAD_DIM
    return dict(
        w_in=w, w_zc=w[:, rwkv_cols:],
        mu_prev=pad_cols(row(mu_prev), ZR_COLS), mu_next=pad_cols(row(mu_next), ZR_COLS),
        w2=jnp.concatenate([_pad_rows(w2_f[l], 128, 0), _pad_rows(w2_b[l], 128, DECAY_LORA)], axis=1).astype(BF16),
        a2=jnp.concatenate([_pad_rows(a2_f[l], 128, 0), _pad_rows(a2_b[l], 128, AAA_LORA)], axis=1).astype(BF16),
        g2=_pad_rows(g2[l], 256, 0).astype(BF16),
        w0=jnp.concatenate([row(w0_f), row(w0_b)], axis=1),
        a0=jnp.concatenate([row(a0_f), row(a0_b)], axis=1),
        k_k=row(k_k), k_a=row(k_a), r_k=row(r_k), gn_w=row(gn_w), gn_b=row(gn_b),
        e=(heads[:, None] == heads[None, :]).astype(BF16),
        dw_w=dw_w[l].astype(F32), dw_b=row(dw_b), cln_w=row(cln_w), cln_b=row(cln_b),
        w_out=w_out[l].astype(BF16), g_post_mix=row(g_post_mix), g_pre_mlp=row(g_pre_mlp),
        w_up=w_up[l].astype(BF16), w_down=w_down[l].astype(BF16), g_post_mlp=row(g_post_mlp),
    )


def _layer(xp, xs, seq_len, p):
    n_seq = (xp.shape[0] + xs.shape[0]) // seq_len
    zr = _inproj(xp, xs, p["w_in"], "inproj_rwkv", cols=ZR_COLS)
    u_glu = _inproj(xp, xs, p["w_zc"], "inproj_conv", glu=True)
    prep = _prep(zr, seq_len, p)
    g, bonus = prep[9], prep[10]
    yf, yb = _wkv(prep[:9], n_seq, seq_len)
    o_rwkv = _post(yf, yb, g, bonus, p)
    o_conv = _conv(u_glu, seq_len, p)
    x1, hm = _outproj(o_rwkv, o_conv, xp, xs, p)
    return _mlp(hm, x1, xp.shape[0], p)


def _forward(x_prompt, x_sample, weights):
    seq_len = x_prompt.shape[1]
    assert x_sample.shape[1] == seq_len
    xp = x_prompt.reshape(-1, D_MODEL)
    xs = x_sample.reshape(-1, D_MODEL)
    depth = weights[0].shape[0]
    for l in range(depth):
        xp, xs = _layer(xp, xs, seq_len, _layer_params(l, *weights))
    return xp.reshape(x_prompt.shape), xs.reshape(x_sample.shape)


def kernel(x_prompt, x_sample, g_pre_mix, w_in, mu_prev, mu_next, w0_f, w2_f, w0_b, w2_b, a0_f, a2_f, a0_b, a2_b, g2, k_k, k_a, r_k, gn_w, gn_b, dw_w, dw_b, cln_w, cln_b, w_out, g_post_mix, g_pre_mlp, w_up, w_down, g_post_mlp):
    weights = (g_pre_mix, w_in, mu_prev, mu_next, w0_f, w2_f, w0_b, w2_b, a0_f, a2_f, a0_b, a2_b,
               g2, k_k, k_a, r_k, gn_w, gn_b, dw_w, dw_b, cln_w, cln_b, w_out, g_post_mix,
               g_pre_mlp, w_up, w_down, g_post_mlp)
    return _forward(x_prompt, x_sample, weights)
```
